```python
import math
import jax, jax.numpy as jnp
from jax import lax
import numpy as np

D_MODEL = 2048
BATCH = 8
SEQ = 2048
DEPTH = 4

HEAD_DIM = 64
N_MIXERS = 4
GROUP_WIDTH = D_MODEL // N_MIXERS
N_HEADS = GROUP_WIDTH // HEAD_DIM
NSA_KV_HEADS = 2
NSA_GQA = N_HEADS // NSA_KV_HEADS
CMP_LEN = 32
CMP_STRIDE = 16
SLC_LEN = 64
SLC_TOP = 16
WINDOW = 512
FORCE_BONUS = 1.0e3
N_BUCKETS = 32
MAX_DISTANCE = 1024
CONV_W = 3
CHUNK = 128
Q_BLOCK = 128
D_FF = -(-8 * D_MODEL // (3 * 256)) * 256
NEG_INF = -1.0e30

A_Q_COLS = N_HEADS * HEAD_DIM
A_KV_COLS = 6 * NSA_KV_HEADS * HEAD_DIM
A_GATE_COLS = 3 * N_HEADS
B_COLS = 3 * GROUP_WIDTH
C_COLS = 2 * GROUP_WIDTH
D_COLS = 3 * GROUP_WIDTH
SPLIT_POINTS = (A_Q_COLS, A_Q_COLS + A_KV_COLS, A_Q_COLS + A_KV_COLS + A_GATE_COLS, A_Q_COLS + A_KV_COLS + A_GATE_COLS + B_COLS, A_Q_COLS + A_KV_COLS + A_GATE_COLS + B_COLS + C_COLS)
W_IN_COLS = SPLIT_POINTS[-1] + D_COLS

kernel_name = 'hybrid_nsa_conv_sgu_stickbreak'


def rms_norm(x, g, eps=1e-6):
    xf = x.astype(jnp.float32)
    y = xf * lax.rsqrt(jnp.mean(xf * xf, axis=-1, keepdims=True) + eps)
    return (y * g.astype(jnp.float32)).astype(x.dtype)


def layer_norm_noaffine(x, eps=1e-5):
    xf = x.astype(jnp.float32)
    mu = jnp.mean(xf, axis=-1, keepdims=True)
    var = jnp.mean(jnp.square(xf - mu), axis=-1, keepdims=True)
    return ((xf - mu) * lax.rsqrt(var + eps)).astype(x.dtype)


def gelu(x):
    return jax.nn.gelu(x, approximate=True)


def rel_bucket(dist):
    n = jnp.maximum(dist, 0)
    max_exact = N_BUCKETS // 2
    nf = jnp.maximum(n, 1).astype(jnp.float32)
    large = max_exact + (jnp.log(nf / max_exact) / math.log(MAX_DISTANCE / max_exact) * (N_BUCKETS - max_exact)).astype(jnp.int32)
    large = jnp.minimum(large, N_BUCKETS - 1)
    return jnp.where(n < max_exact, n, large)


def rel_bias_heads(dist, table):
    b = table[rel_bucket(dist)].astype(jnp.float32)
    b = jnp.moveaxis(b, -1, 0)
    return b.reshape(NSA_KV_HEADS, NSA_GQA, *dist.shape)


def masked_softmax(s, mask):
    s = jnp.where(mask, s, NEG_INF)
    p = jax.nn.softmax(s, axis=-1)
    return jnp.where(mask, p, 0.0)


def cmp_slc_overlap(n_cmp, n_slc):
    c0 = np.arange(n_cmp)[:, None] * CMP_STRIDE
    s0 = np.arange(n_slc)[None, :] * SLC_LEN
    ov = np.minimum(c0 + CMP_LEN, s0 + SLC_LEN) - np.maximum(c0, s0)
    return (np.maximum(ov, 0) / CMP_LEN).astype(np.float32)


def nsa_attention(q, k_c, v_c, k_s, v_s, k_w, v_w, gates, q_gain, k_gain, cmp_pos, cmp_w1, cmp_w2, rel_table):
    B, T = q.shape[0], q.shape[1]
    G, R, Dh = NSA_KV_HEADS, NSA_GQA, HEAD_DIM
    scale = Dh ** -0.5
    q = rms_norm(q, q_gain).reshape(B, T, G, R, Dh)
    k_s = rms_norm(k_s, k_gain)
    k_w = rms_norm(k_w, k_gain)
    t_pos = jnp.arange(T, dtype=jnp.int32)

    n_cmp = (T - CMP_LEN) // CMP_STRIDE + 1
    blk = np.arange(n_cmp)[:, None] * CMP_STRIDE + np.arange(CMP_LEN)[None, :]

    def compress(z, i):
        zb = z[:, blk] + cmp_pos[i][None, None, :, None, :]
        zb = zb.transpose(0, 1, 3, 2, 4).reshape(B, n_cmp, G, CMP_LEN * Dh)
        return gelu(zb @ cmp_w1[i]) @ cmp_w2[i]

    kc = rms_norm(compress(k_c, 0), k_gain)
    vc = compress(v_c, 1)
    cmp_end = jnp.asarray(blk[:, -1], jnp.int32)
    dist_c = t_pos[:, None] - cmp_end[None, :]
    s_c = jnp.einsum('btgrd,bngd->bgrtn', q, kc).astype(jnp.float32) * scale + rel_bias_heads(dist_c, rel_table)
    p_c = masked_softmax(s_c, dist_c >= 0)
    o_cmp = jnp.einsum('bgrtn,bngd->btgrd', p_c.astype(vc.dtype), vc)

    n_slc = T // SLC_LEN
    top = min(SLC_TOP, n_slc)
    imp = jnp.einsum('bgrtn,nj->bgtj', p_c, jnp.asarray(cmp_slc_overlap(n_cmp, n_slc)))
    j_idx = jnp.arange(n_slc, dtype=jnp.int32)
    cur = t_pos // SLC_LEN
    valid = (j_idx[None, :] * SLC_LEN) <= t_pos[:, None]
    forced = (j_idx[None, :] == 0) | (j_idx[None, :] == cur[:, None]) | (j_idx[None, :] == cur[:, None] - 1)
    score = jnp.where(valid, imp + jnp.where(forced, FORCE_BONUS, 0.0), NEG_INF)
    _, sel = lax.top_k(score, top)

    ks_blk = k_s.reshape(B, n_slc, SLC_LEN, G, Dh).transpose(0, 3, 1, 2, 4)
    vs_blk = v_s.reshape(B, n_slc, SLC_LEN, G, Dh).transpose(0, 3, 1, 2, 4)
    pad = ((0, 0), (WINDOW, 0), (0, 0), (0, 0))
    kw_pad = jnp.pad(k_w, pad)
    vw_pad = jnp.pad(v_w, pad)
    table_g = rel_table.reshape(N_BUCKETS, G, R)
    g_idx = jnp.arange(G)[None, :, None, None, None]
    gather_blocks = jax.vmap(jax.vmap(lambda blocks, ix: blocks[ix]))
    win_len = WINDOW + Q_BLOCK

    def block_fn(qb):
        qs = qb * Q_BLOCK
        qt = qs + jnp.arange(Q_BLOCK, dtype=jnp.int32)
        qblk = lax.dynamic_slice_in_dim(q, qs, Q_BLOCK, axis=1)
        ix = lax.dynamic_slice_in_dim(sel, qs, Q_BLOCK, axis=2)
        ksel = gather_blocks(ks_blk, ix)
        vsel = gather_blocks(vs_blk, ix)
        kpos = ix[..., None] * SLC_LEN + jnp.arange(SLC_LEN, dtype=jnp.int32)
        d_s = qt[None, None, :, None, None] - kpos
        b_s = jnp.moveaxis(table_g[rel_bucket(d_s), g_idx].astype(jnp.float32), -1, 2)
        s_s = jnp.einsum('btgrd,bgtkld->bgrtkl', qblk, ksel).astype(jnp.float32) * scale + b_s
        shp = s_s.shape
        p_s = masked_softmax(s_s.reshape(B, G, R, Q_BLOCK, -1), (d_s >= 0).reshape(B, G, 1, Q_BLOCK, -1)).reshape(shp)
        o_s = jnp.einsum('bgrtkl,bgtkld->btgrd', p_s.astype(vsel.dtype), vsel)
        kw = lax.dynamic_slice_in_dim(kw_pad, qs, win_len, axis=1)
        vw = lax.dynamic_slice_in_dim(vw_pad, qs, win_len, axis=1)
        kp = qs - WINDOW + jnp.arange(win_len, dtype=jnp.int32)
        d_w = qt[:, None] - kp[None, :]
        m_w = (d_w >= 0) & (d_w < WINDOW) & (kp[None, :] >= 0)
        s_w = jnp.einsum('btgrd,bsgd->bgrts', qblk, kw).astype(jnp.float32) * scale + rel_bias_heads(d_w, rel_table)
        p_w = masked_softmax(s_w, m_w)
        o_w = jnp.einsum('bgrts,bsgd->btgrd', p_w.astype(vw.dtype), vw)
        return o_s, o_w

    o_slc, o_win = lax.map(block_fn, jnp.arange(T // Q_BLOCK, dtype=jnp.int32))
    o_slc = jnp.moveaxis(o_slc, 0, 1).reshape(B, T, G, R, Dh)
    o_win = jnp.moveaxis(o_win, 0, 1).reshape(B, T, G, R, Dh)

    g = jax.nn.sigmoid(gates.astype(jnp.float32)).astype(q.dtype).reshape(B, T, G, R, 3)
    o = g[..., 0:1] * o_cmp + g[..., 1:2] * o_slc + g[..., 2:3] * o_win
    return o.reshape(B, T, G * R * Dh)


def short_conv_mixer(cols, conv_w):
    b_gate, c_gate, h = jnp.split(cols, 3, axis=-1)
    z = c_gate * h
    T = z.shape[1]
    zp = jnp.pad(z, ((0, 0), (CONV_W - 1, 0), (0, 0)))
    y = conv_w[0] * zp[:, 0:T]
    for i in range(1, CONV_W):
        y = y + conv_w[i] * zp[:, i:i + T]
    return b_gate * y


def spatial_gating_mixer(cols, sgu_w, sgu_b):
    B, T = cols.shape[0], cols.shape[1]
    u, v = jnp.split(gelu(cols), 2, axis=-1)
    v = layer_norm_noaffine(v).reshape(B, T // CHUNK, CHUNK, N_HEADS, HEAD_DIM)
    w = sgu_w * jnp.asarray(np.tril(np.ones((CHUNK, CHUNK), np.float32)), sgu_w.dtype)
    s = jnp.einsum('hpq,bcqhe->bcphe', w, v) + sgu_b.T[None, None, :, :, None]
    return u * s.reshape(B, T, GROUP_WIDTH)


def stick_breaking_attention(q, k, v):
    B, T, H, Dh = q.shape
    scale = Dh ** -0.5
    outs = []
    for qb in range(T // Q_BLOCK):
        qs, qe = qb * Q_BLOCK, (qb + 1) * Q_BLOCK
        z = jnp.einsum('bthd,bshd->bhts', q[:, qs:qe], k[:, :qe]).astype(jnp.float32) * scale
        mask = jnp.asarray(np.arange(qe)[None, :] < np.arange(qs, qe)[:, None])
        log_beta = jax.nn.log_sigmoid(z)
        log_1m = jnp.where(mask, log_beta - z, 0.0)
        tail = lax.cumsum(log_1m, axis=3, reverse=True) - log_1m
        a = jnp.where(mask, jnp.exp(log_beta + tail), 0.0)
        outs.append(jnp.einsum('bhts,bshd->bthd', a.astype(v.dtype), v[:, :qe]))
    return jnp.concatenate(outs, axis=1).reshape(B, T, H * Dh)


def setup_inputs(seed: int = 0) -> dict:
    key = jax.random.key(seed)
    ks = jax.random.split(key, 18)
    f32 = jnp.float32

    def nrm(k, shape, scale):
        return jax.random.normal(k, shape, f32) * scale

    res_scale = (2 * DEPTH) ** -0.5
    return {
        'x': nrm(ks[0], (BATCH, SEQ, D_MODEL), 1.0),
        'w_in': nrm(ks[1], (DEPTH, D_MODEL, W_IN_COLS), D_MODEL ** -0.5),
        'w_out': nrm(ks[2], (DEPTH, D_MODEL, D_MODEL), D_MODEL ** -0.5 * res_scale),
        'norm_mix': 1.0 + nrm(ks[3], (DEPTH, D_MODEL), 0.05),
        'norm_ffn': 1.0 + nrm(ks[4], (DEPTH, D_MODEL), 0.05),
        'q_gain': 1.0 + nrm(ks[5], (DEPTH, HEAD_DIM), 0.05),
        'k_gain': 1.0 + nrm(ks[6], (DEPTH, HEAD_DIM), 0.05),
        'cmp_pos': nrm(ks[7], (DEPTH, 2, CMP_LEN, HEAD_DIM), 0.1),
        'cmp_w1': nrm(ks[8], (DEPTH, 2, CMP_LEN * HEAD_DIM, HEAD_DIM), (CMP_LEN * HEAD_DIM) ** -0.5),
        'cmp_w2': nrm(ks[9], (DEPTH, 2, HEAD_DIM, HEAD_DIM), HEAD_DIM ** -0.5),
        'rel_table': nrm(ks[10], (N_BUCKETS, N_HEADS), 0.5),
        'conv_w': nrm(ks[11], (DEPTH, CONV_W, GROUP_WIDTH), CONV_W ** -0.5),
        'sgu_w': nrm(ks[12], (DEPTH, N_HEADS, CHUNK, CHUNK), CHUNK ** -0.5),
        'sgu_b': 1.0 + nrm(ks[13], (DEPTH, N_HEADS, CHUNK), 0.1),
        'group_gain': 1.0 + nrm(ks[14], (DEPTH, D_MODEL), 0.05),
        'w_ffn_gate': nrm(ks[15], (DEPTH, D_MODEL, D_FF), D_MODEL ** -0.5),
        'w_ffn_up': nrm(ks[16], (DEPTH, D_MODEL, D_FF), D_MODEL ** -0.5),
        'w_ffn_down': nrm(ks[17], (DEPTH, D_FF, D_MODEL), D_FF ** -0.5 * res_scale),
    }


def reference(x, w_in, w_out, norm_mix, norm_ffn, q_gain, k_gain, cmp_pos, cmp_w1, cmp_w2, rel_table, conv_w, sgu_w, sgu_b, group_gain, w_ffn_gate, w_ffn_up, w_ffn_down):
    B, T = x.shape[0], x.shape[1]
    for l in range(DEPTH):
        h = rms_norm(x, norm_mix[l])
        proj = h @ w_in[l]
        a_q, a_kv, a_g, b_cols, c_cols, d_cols = jnp.split(proj, SPLIT_POINTS, axis=-1)
        kv = a_kv.reshape(B, T, 6, NSA_KV_HEADS, HEAD_DIM)
        o_a = nsa_attention(a_q.reshape(B, T, N_HEADS, HEAD_DIM), kv[:, :, 0], kv[:, :, 1], kv[:, :, 2], kv[:, :, 3], kv[:, :, 4], kv[:, :, 5], a_g.reshape(B, T, N_HEADS, 3), q_gain[l], k_gain[l], cmp_pos[l], cmp_w1[l], cmp_w2[l], rel_table)
        o_b = short_conv_mixer(b_cols, conv_w[l])
        o_c = spatial_gating_mixer(c_cols, sgu_w[l], sgu_b[l])
        dq, dk, dv = jnp.split(d_cols, 3, axis=-1)
        o_d = stick_breaking_attention(dq.reshape(B, T, N_HEADS, HEAD_DIM), dk.reshape(B, T, N_HEADS, HEAD_DIM), dv.reshape(B, T, N_HEADS, HEAD_DIM))
        mixed = jnp.stack([o_a, o_b, o_c, o_d], axis=2)
        mixed = rms_norm(mixed, group_gain[l].reshape(N_MIXERS, GROUP_WIDTH))
        x = x + mixed.reshape(B, T, D_MODEL) @ w_out[l]
        h = rms_norm(x, norm_ffn[l])
        x = x + (jax.nn.silu(h @ w_ffn_gate[l]) * (h @ w_ffn_up[l])) @ w_ffn_down[l]
    return x
```

```python
import functools
import math

import numpy as np
import jax
import jax.numpy as jnp
from jax import lax
from jax.experimental import pallas as pl
from jax.experimental.pallas import tpu as pltpu

F32 = jnp.float32
BF16 = jnp.bfloat16

D_MODEL = 2048
SEQ = 2048
HEAD_DIM = 64
GROUP_WIDTH = 512
N_HEADS = 8
NSA_KV_HEADS = 2
NSA_GQA = 4
CMP_LEN = 32
CMP_STRIDE = 16
SLC_LEN = 64
SLC_TOP = 16
WINDOW = 512
FORCE_BONUS = 1.0e3
N_BUCKETS = 32
MAX_DISTANCE = 1024
CONV_W = 3
CHUNK = 128
D_FF = 5632
NEG_INF = -1.0e30
N_CMP = (SEQ - CMP_LEN) // CMP_STRIDE + 1
N_SLC = SEQ // SLC_LEN
W_IN_COLS = 5400

LANES = 128
VMEM_LIMIT = 56 * 1024 * 1024

COL_Q = 0
COL_BG, COL_BC, COL_BH = 512, 1024, 1536
COL_CU, COL_CV = 2048, 2560
COL_DQ, COL_DK, COL_DV = 3072, 3584, 4096
COL_KV = 4608
COL_GATE = 5376
PROJ_COLS = 5632

TM_IN, TN_IN = 512, 1408
TQ = 128
TK = 128
TQ_SB = 256
TM_BC = 256
TM_OUT, TN_OUT = 512, 1024
TM_FFN, TF_FFN = 512, 512


def _params(*sem):
    return pltpu.CompilerParams(dimension_semantics=sem, vmem_limit_bytes=VMEM_LIMIT)


def _dot(a, b):
    return jnp.dot(a, b, preferred_element_type=F32)


def _dot_nt(a, b):
    return lax.dot_general(a, b, (((1,), (1,)), ((), ())), preferred_element_type=F32)


def _split_dot(a, b, terms):
    out = None
    rem = a
    for i in range(terms):
        part = rem.astype(BF16)
        d = _dot(part, b)
        out = d if out is None else out + d
        if i + 1 < terms:
            rem = rem - part.astype(F32)
    return out


def _gelu(x):
    c = math.sqrt(2.0 / math.pi)
    return x * (0.5 * (1.0 + jnp.tanh(c * (x + 0.044715 * (x * x * x)))))


def _rel_bucket_np(dist):
    n = np.maximum(dist, 0)
    max_exact = N_BUCKETS // 2
    nf = np.maximum(n, 1).astype(np.float64)
    large = max_exact + (np.log(nf / max_exact) / math.log(MAX_DISTANCE / max_exact)
                         * (N_BUCKETS - max_exact)).astype(np.int32)
    large = np.minimum(large, N_BUCKETS - 1)
    return np.where(n < max_exact, n, large).astype(np.int32)


def _bias_expand_kernel(tab_ref, bkt_ref, o_ref):
    bkt = bkt_ref[...]
    for h in range(N_HEADS):
        acc = jnp.zeros(bkt.shape, F32)
        for b in range(N_BUCKETS):
            acc = jnp.where(bkt == b, tab_ref[b, h], acc)
        o_ref[h] = acc


def _bias_expand(rel_table, bucket_np):
    rows = bucket_np.shape[0]
    tr = 256
    return pl.pallas_call(
        _bias_expand_kernel,
        grid=(rows // tr,),
        in_specs=[pl.BlockSpec(memory_space=pltpu.SMEM),
                  pl.BlockSpec((tr, LANES), lambda i: (i, 0))],
        out_specs=pl.BlockSpec((N_HEADS, tr, LANES), lambda i: (0, i, 0)),
        out_shape=jax.ShapeDtypeStruct((N_HEADS, rows, LANES), F32),
        compiler_params=_params("parallel"),
        name="bias_expand",
    )(rel_table, jnp.asarray(bucket_np))


def _inproj_kernel(x_ref, g_ref, w_ref, o_ref, h_ref):
    @pl.when(pl.program_id(1) == 0)
    def _():
        x = x_ref[...]
        ms = jnp.mean(x * x, axis=-1, keepdims=True)
        h_ref[...] = ((x * lax.rsqrt(ms + 1e-6)) * g_ref[...]).astype(BF16)

    o_ref[...] = _dot(h_ref[...], w_ref[...])


def _inproj(x2, gain, w):
    m = x2.shape[0]
    return pl.pallas_call(
        _inproj_kernel,
        grid=(m // TM_IN, PROJ_COLS // TN_IN),
        in_specs=[pl.BlockSpec((TM_IN, D_MODEL), lambda i, j: (i, 0)),
                  pl.BlockSpec((1, D_MODEL), lambda i, j: (0, 0)),
                  pl.BlockSpec((D_MODEL, TN_IN), lambda i, j: (0, j))],
        out_specs=pl.BlockSpec((TM_IN, TN_IN), lambda i, j: (i, j)),
        out_shape=jax.ShapeDtypeStruct((m, PROJ_COLS), F32),
        scratch_shapes=[pltpu.VMEM((TM_IN, D_MODEL), BF16)],
        compiler_params=_params("parallel", "arbitrary"),
        name="inproj",
    )(x2, gain, w)


def _seg_rms(x, gain2):
    lane = lax.broadcasted_iota(jnp.int32, x.shape, 1)
    x2 = x * x
    lo = lane < HEAD_DIM
    s0 = jnp.sum(jnp.where(lo, x2, 0.0), axis=-1, keepdims=True)
    s1 = jnp.sum(jnp.where(lo, 0.0, x2), axis=-1, keepdims=True)
    ms = jnp.where(lo, s0, s1) * (1.0 / HEAD_DIM)
    return (x * lax.rsqrt(ms + 1e-6)) * gain2


def _nsa_prep_kernel(kv_ref, hb_ref, w1_ref, w2_ref, pos_ref, kg_ref, kg2_ref,
                     kc_ref, vc_ref, ks_ref, vs_ref, kw_ref, vw_ref):
    kg2 = kg2_ref[...]
    ks_ref[0] = _seg_rms(kv_ref[:, 2 * LANES:3 * LANES], kg2).astype(BF16)
    vs_ref[0] = kv_ref[:, 3 * LANES:4 * LANES].astype(BF16)
    kw_ref[0] = _seg_rms(kv_ref[:, 4 * LANES:5 * LANES], kg2).astype(BF16)
    vw_ref[0] = kv_ref[:, 5 * LANES:6 * LANES].astype(BF16)

    half = CMP_STRIDE * HEAD_DIM
    for i, out_ref in ((0, kc_ref), (1, vc_ref)):
        w1 = w1_ref[i]
        pos = jnp.broadcast_to(pos_ref[i], (8, CMP_LEN * HEAD_DIM)).astype(BF16)
        posterm = _dot(pos, w1)[0:1]
        for g in range(NSA_KV_HEADS):
            hb = hb_ref[0, i, g]
            p1 = _dot(hb, w1[:half])
            p2 = _dot(hb, w1[half:])
            pre = p1 + pltpu.roll(p2, SEQ // CMP_STRIDE - 1, axis=0) + posterm
            out = _dot(_gelu(pre).astype(BF16), w2_ref[i])
            if i == 0:
                ms = jnp.mean(out * out, axis=-1, keepdims=True)
                out = (out * lax.rsqrt(ms + 1e-6)) * kg_ref[...]
            out_ref[0, g] = out.astype(BF16)


def _nsa_prep(proj, hb, w1, w2, pos, kgain, kgain2, batch):
    nhb = SEQ // CMP_STRIDE
    cmp_shape = jax.ShapeDtypeStruct((batch, NSA_KV_HEADS, nhb, HEAD_DIM), BF16)
    tok_shape = jax.ShapeDtypeStruct((batch, SEQ, LANES), BF16)
    cmp_spec = pl.BlockSpec((1, NSA_KV_HEADS, nhb, HEAD_DIM), lambda b: (b, 0, 0, 0))
    tok_spec = pl.BlockSpec((1, SEQ, LANES), lambda b: (b, 0, 0))
    return pl.pallas_call(
        _nsa_prep_kernel,
        grid=(batch,),
        in_specs=[pl.BlockSpec((SEQ, 6 * LANES), lambda b: (b, COL_KV // (6 * LANES))),
                  pl.BlockSpec((1, 2, NSA_KV_HEADS, nhb, CMP_STRIDE * HEAD_DIM), lambda b: (b, 0, 0, 0, 0)),
                  pl.BlockSpec((2, CMP_LEN * HEAD_DIM, HEAD_DIM), lambda b: (0, 0, 0)),
                  pl.BlockSpec((2, HEAD_DIM, HEAD_DIM), lambda b: (0, 0, 0)),
                  pl.BlockSpec((2, 1, CMP_LEN * HEAD_DIM), lambda b: (0, 0, 0)),
                  pl.BlockSpec((1, HEAD_DIM), lambda b: (0, 0)),
                  pl.BlockSpec((1, LANES), lambda b: (0, 0))],
        out_specs=[cmp_spec, cmp_spec, tok_spec, tok_spec, tok_spec, tok_spec],
        out_shape=[cmp_shape, cmp_shape, tok_shape, tok_shape, tok_shape, tok_shape],
        compiler_params=_params("parallel"),
        name="nsa_prep",
    )(proj, hb, w1, w2, pos, kgain, kgain2)


def _nsa_kernel(q_ref, gate_ref, kc_ref, vc_ref, ks_ref, vs_ref, kw_ref, vw_ref,
                bc_ref, bt_ref, ov_ref, qg_ref, o_ref):
    qi = pl.program_id(1)
    rows = NSA_GQA * TQ
    q = q_ref[...]
    qg = qg_ref[...] * (HEAD_DIM ** -0.5)
    qn = []
    for h in range(N_HEADS):
        seg = q[:, h * HEAD_DIM:(h + 1) * HEAD_DIM]
        ms = jnp.mean(seg * seg, axis=-1, keepdims=True)
        qn.append(((seg * lax.rsqrt(ms + 1e-6)) * qg).astype(BF16))
    gates = jax.nn.sigmoid(gate_ref[...])

    t1 = qi * TQ + lax.broadcasted_iota(jnp.int32, (TQ, LANES), 0)
    t4 = jnp.concatenate([t1] * NSA_GQA, axis=0)
    lane1 = lax.broadcasted_iota(jnp.int32, (TQ, LANES), 1)
    lane4 = lax.broadcasted_iota(jnp.int32, (rows, LANES), 1)
    blk_row = lax.broadcasted_iota(jnp.int32, (LANES, TK), 0)
    key_col = lax.broadcasted_iota(jnp.int32, (LANES, TK), 1) // SLC_LEN

    outs = []
    for g in range(NSA_KV_HEADS):
        qst = jnp.concatenate(qn[g * NSA_GQA:(g + 1) * NSA_GQA], axis=0)
        glo, ghi = g * HEAD_DIM, (g + 1) * HEAD_DIM

        s = _dot_nt(qst, kc_ref[0, g]) + bc_ref[g * NSA_GQA:(g + 1) * NSA_GQA].reshape(rows, LANES)
        mask_c = (t4 - (lane4 * CMP_STRIDE + (CMP_LEN - 1)) >= 0) & (lane4 < N_CMP)
        s = jnp.where(mask_c, s, NEG_INF)
        m = jnp.max(s, axis=-1, keepdims=True)
        e = jnp.where(mask_c, jnp.exp(s - m), 0.0)
        l = jnp.sum(e, axis=-1, keepdims=True)
        p = e * jnp.where(l > 0.0, 1.0 / l, 0.0)
        o_cmp = _dot(p.astype(BF16), vc_ref[0, g])

        psum = p[0:TQ] + p[TQ:2 * TQ] + p[2 * TQ:3 * TQ] + p[3 * TQ:4 * TQ]
        imp = _split_dot(psum, ov_ref[...], 3)
        cur = t1 // SLC_LEN
        valid = (lane1 * SLC_LEN <= t1) & (lane1 < N_SLC)
        forced = (lane1 == 0) | (lane1 == cur) | (lane1 == cur - 1)
        score = jnp.where(valid, imp + jnp.where(forced, FORCE_BONUS, 0.0), NEG_INF)
        score = jnp.where(lane1 < N_SLC, score, -3.0e38)
        rank = jnp.zeros((TQ, LANES), F32)
        for k in range(N_SLC):
            ck = score[:, k:k + 1]
            before = (ck > score) | ((ck == score) & (lane1 > k))
            rank = rank + jnp.where(before, 1.0, 0.0)
        sel = jnp.where((rank < float(SLC_TOP)) & valid, 1.0, 0.0).astype(BF16)

        def slc_body(kj, carry):
            m_i, l_i, acc = carry
            off = pl.multiple_of(kj * TK, TK)
            kt = ks_ref[0, pl.ds(off, TK), :][:, glo:ghi]
            vt = vs_ref[0, pl.ds(off, TK), :][:, glo:ghi]
            bias = bt_ref[qi - kj, g * NSA_GQA:(g + 1) * NSA_GQA].reshape(rows, TK)
            sc = _dot_nt(qst, kt) + bias
            expand = jnp.where(blk_row == key_col + 2 * kj, 1.0, 0.0).astype(BF16)
            msel = _dot(sel, expand)
            ok1 = (msel > 0.5) & (t1 >= lane1 + off)
            okf = jnp.where(ok1, 1.0, 0.0)
            ok = jnp.concatenate([okf] * NSA_GQA, axis=0) > 0.5
            sc = jnp.where(ok, sc, NEG_INF)
            m_new = jnp.maximum(m_i, jnp.max(sc, axis=-1, keepdims=True))
            alpha = jnp.exp(m_i - m_new)
            pe = jnp.where(ok, jnp.exp(sc - m_new), 0.0)
            l_new = alpha * l_i + jnp.sum(pe, axis=-1, keepdims=True)
            acc_new = alpha * acc + _dot(pe.astype(BF16), vt)
            return m_new, l_new, acc_new

        init = (jnp.full((rows, 1), NEG_INF, F32), jnp.zeros((rows, 1), F32), jnp.zeros((rows, HEAD_DIM), F32))
        _, l_s, acc_s = lax.fori_loop(0, qi + 1, slc_body, init)
        o_slc = acc_s / l_s

        def win_body(kj, carry):
            m_i, l_i, acc = carry
            off = pl.multiple_of(kj * TK, TK)
            kt = kw_ref[0, pl.ds(off, TK), :][:, glo:ghi]
            vt = vw_ref[0, pl.ds(off, TK), :][:, glo:ghi]
            bias = bt_ref[qi - kj, g * NSA_GQA:(g + 1) * NSA_GQA].reshape(rows, TK)
            sc = _dot_nt(qst, kt) + bias
            dist = t4 - (lane4 + off)
            ok = (dist >= 0) & (dist < WINDOW)
            sc = jnp.where(ok, sc, NEG_INF)
            m_new = jnp.maximum(m_i, jnp.max(sc, axis=-1, keepdims=True))
            alpha = jnp.exp(m_i - m_new)
            pe = jnp.where(ok, jnp.exp(sc - m_new), 0.0)
            l_new = alpha * l_i + jnp.sum(pe, axis=-1, keepdims=True)
            acc_new = alpha * acc + _dot(pe.astype(BF16), vt)
            return m_new, l_new, acc_new

        _, l_w, acc_w = lax.fori_loop(jnp.maximum(qi - WINDOW // TK, 0), qi + 1, win_body, init)
        o_win = acc_w / l_w

        for r in range(NSA_GQA):
            h = g * NSA_GQA + r
            rs = slice(r * TQ, (r + 1) * TQ)
            outs.append(gates[:, 3 * h:3 * h + 1] * o_cmp[rs]
                        + gates[:, 3 * h + 1:3 * h + 2] * o_slc[rs]
                        + gates[:, 3 * h + 2:3 * h + 3] * o_win[rs])
    o_ref[...] = jnp.concatenate(outs, axis=-1)


def _nsa_attention(proj, kc, vc, ks, vs, kw, vw, bias_c, bias_t, overlap, qgain, batch):
    nq = SEQ // TQ
    nhb = SEQ // CMP_STRIDE
    cmp_spec = pl.BlockSpec((1, NSA_KV_HEADS, nhb, HEAD_DIM), lambda b, i: (b, 0, 0, 0))
    tok_spec = pl.BlockSpec((1, SEQ, LANES), lambda b, i: (b, 0, 0))
    return pl.pallas_call(
        _nsa_kernel,
        grid=(batch, nq),
        in_specs=[pl.BlockSpec((TQ, GROUP_WIDTH), lambda b, i: (b * nq + i, COL_Q // GROUP_WIDTH)),
                  pl.BlockSpec((TQ, LANES), lambda b, i: (b * nq + i, COL_GATE // LANES)),
                  cmp_spec, cmp_spec, tok_spec, tok_spec, tok_spec, tok_spec,
                  pl.BlockSpec((N_HEADS, TQ, LANES), lambda b, i: (0, i, 0)),
                  pl.BlockSpec((SEQ // TK, N_HEADS, TQ, TK), lambda b, i: (0, 0, 0, 0)),
                  pl.BlockSpec((LANES, LANES), lambda b, i: (0, 0)),
                  pl.BlockSpec((1, HEAD_DIM), lambda b, i: (0, 0))],
        out_specs=pl.BlockSpec((TQ, GROUP_WIDTH), lambda b, i: (b * nq + i, 0)),
        out_shape=jax.ShapeDtypeStruct((batch * SEQ, GROUP_WIDTH), F32),
        compiler_params=_params("parallel", "arbitrary"),
        name="nsa_attention",
    )(proj, proj, kc, vc, ks, vs, kw, vw, bias_c, bias_t, overlap, qgain)


def _stickbreak_kernel(q_ref, k_ref, v_ref, tri_ref, o_ref):
    qi = pl.program_id(1)
    q = q_ref[...] * (HEAD_DIM ** -0.5)
    t = qi * TQ_SB + lax.broadcasted_iota(jnp.int32, (TQ_SB, TK), 0)
    lane = lax.broadcasted_iota(jnp.int32, (TQ_SB, TK), 1)
    n_kt = (qi + 1) * (TQ_SB // TK)
    tri = tri_ref[...]
    outs = []
    for h in range(N_HEADS):
        lo, hi = h * HEAD_DIM, (h + 1) * HEAD_DIM
        qh = q[:, lo:hi].astype(BF16)

        def body(i, carry):
            tail_c, acc = carry
            kj = n_kt - 1 - i
            off = pl.multiple_of(kj * TK, TK)
            kt = k_ref[pl.ds(off, TK), :][:, lo:hi].astype(BF16)
            vt = v_ref[pl.ds(off, TK), :][:, lo:hi].astype(BF16)
            z = _dot_nt(qh, kt)
            ok = (lane + off) < t
            log_beta = jnp.minimum(z, 0.0) - jnp.log1p(jnp.exp(-jnp.abs(z)))
            log_1m = jnp.where(ok, log_beta - z, 0.0)
            tail = _split_dot(log_1m, tri, 2) + tail_c
            a = jnp.where(ok, jnp.exp(log_beta + tail), 0.0)
            acc = acc + _dot(a.astype(BF16), vt)
            tail_c = tail_c + jnp.sum(log_1m, axis=-1, keepdims=True)
            return tail_c, acc

        init = (jnp.zeros((TQ_SB, 1), F32), jnp.zeros((TQ_SB, HEAD_DIM), F32))
        _, acc = lax.fori_loop(0, n_kt, body, init)
        outs.append(acc)
    o_ref[...] = jnp.concatenate(outs, axis=-1)


def _stickbreak(proj, tri, batch):
    nq = SEQ // TQ_SB
    return pl.pallas_call(
        _stickbreak_kernel,
        grid=(batch, nq),
        in_specs=[pl.BlockSpec((TQ_SB, GROUP_WIDTH), lambda b, i: (b * nq + i, COL_DQ // GROUP_WIDTH)),
                  pl.BlockSpec((SEQ, GROUP_WIDTH), lambda b, i: (b, COL_DK // GROUP_WIDTH)),
                  pl.BlockSpec((SEQ, GROUP_WIDTH), lambda b, i: (b, COL_DV // GROUP_WIDTH)),
                  pl.BlockSpec((TK, TK), lambda b, i: (0, 0))],
        out_specs=pl.BlockSpec((TQ_SB, GROUP_WIDTH), lambda b, i: (b * nq + i, 0)),
        out_shape=jax.ShapeDtypeStruct((batch * SEQ, GROUP_WIDTH), F32),
        compiler_params=_params("parallel", "arbitrary"),
        name="stickbreak",
    )(proj, proj, proj, tri)


def _conv_sgu_kernel(bg_ref, cg_ref, hh_ref, cgp_ref, hhp_ref, cu_ref, cv_ref, cw_ref, sw_ref, sb_ref,
                     ob_ref, oc_ref):
    i = pl.program_id(0)
    z = cg_ref[...] * hh_ref[...]
    first = (i % (SEQ // TM_BC)) == 0
    zp = jnp.where(first, 0.0, cgp_ref[...] * hhp_ref[...])
    row = lax.broadcasted_iota(jnp.int32, z.shape, 0)
    p1 = jnp.broadcast_to(zp[7:8], z.shape)
    p2 = jnp.broadcast_to(zp[6:7], z.shape)
    z1 = jnp.where(row == 0, p1, pltpu.roll(z, 1, axis=0))
    z2 = jnp.where(row == 0, p2, jnp.where(row == 1, p1, pltpu.roll(z, 2, axis=0)))
    cw = cw_ref[...]
    y = cw[0:1] * z2
    y = y + cw[1:2] * z1
    y = y + cw[2:3] * z
    ob_ref[...] = bg_ref[...] * y

    u = _gelu(cu_ref[...])
    v = _gelu(cv_ref[...])
    mu = jnp.mean(v, axis=-1, keepdims=True)
    vc = v - mu
    var = jnp.mean(vc * vc, axis=-1, keepdims=True)
    vn = (vc * lax.rsqrt(var + 1e-5)).astype(BF16)
    pr = lax.broadcasted_iota(jnp.int32, (CHUNK, CHUNK), 0)
    pc = lax.broadcasted_iota(jnp.int32, (CHUNK, CHUNK), 1)
    ws = [jnp.where(pr >= pc, sw_ref[h], 0.0).astype(BF16) for h in range(N_HEADS)]
    for c in range(TM_BC // CHUNK):
        rs = slice(c * CHUNK, (c + 1) * CHUNK)
        s = jnp.concatenate(
            [_dot(ws[h], vn[rs, h * HEAD_DIM:(h + 1) * HEAD_DIM]) for h in range(N_HEADS)], axis=-1)
        oc_ref[rs, :] = u[rs] * (s + sb_ref[...])


def _conv_sgu(proj, conv_w, sgu_w, sgu_bx):
    m = proj.shape[0]
    gw = GROUP_WIDTH

    def col(c):
        return pl.BlockSpec((TM_BC, gw), lambda i: (i, c // gw))

    def prev(c):
        return pl.BlockSpec((8, gw), lambda i: (jnp.maximum(i * (TM_BC // 8) - 1, 0), c // gw))

    out = jax.ShapeDtypeStruct((m, gw), F32)
    return pl.pallas_call(
        _conv_sgu_kernel,
        grid=(m // TM_BC,),
        in_specs=[col(COL_BG), col(COL_BC), col(COL_BH), prev(COL_BC), prev(COL_BH), col(COL_CU), col(COL_CV),
                  pl.BlockSpec((CONV_W, gw), lambda i: (0, 0)),
                  pl.BlockSpec((N_HEADS, CHUNK, CHUNK), lambda i: (0, 0, 0)),
                  pl.BlockSpec((CHUNK, gw), lambda i: (0, 0))],
        out_specs=[pl.BlockSpec((TM_BC, gw), lambda i: (i, 0))] * 2,
        out_shape=[out, out],
        compiler_params=_params("parallel"),
        name="conv_sgu",
    )(proj, proj, proj, proj, proj, proj, proj, conv_w, sgu_w, sgu_bx)


def _outproj_kernel(oa_ref, ob_ref, oc_ref, od_ref, gg_ref, w_ref, x_ref, o_ref, mix_ref):
    @pl.when(pl.program_id(1) == 0)
    def _():
        for i, ref in enumerate((oa_ref, ob_ref, oc_ref, od_ref)):
            o = ref[...]
            ms = jnp.mean(o * o, axis=-1, keepdims=True)
            mix_ref[:, i * GROUP_WIDTH:(i + 1) * GROUP_WIDTH] = (
                (o * lax.rsqrt(ms + 1e-6)) * gg_ref[i:i + 1]).astype(BF16)

    o_ref[...] = x_ref[...] + _dot(mix_ref[...], w_ref[...])


def _outproj(oa, ob, oc, od, gg, w, x2):
    m = x2.shape[0]
    mix_spec = pl.BlockSpec((TM_OUT, GROUP_WIDTH), lambda i, j: (i, 0))
    return pl.pallas_call(
        _outproj_kernel,
        grid=(m // TM_OUT, D_MODEL // TN_OUT),
        in_specs=[mix_spec, mix_spec, mix_spec, mix_spec,
                  pl.BlockSpec((4, GROUP_WIDTH), lambda i, j: (0, 0)),
                  pl.BlockSpec((D_MODEL, TN_OUT), lambda i, j: (0, j)),
                  pl.BlockSpec((TM_OUT, TN_OUT), lambda i, j: (i, j))],
        out_specs=pl.BlockSpec((TM_OUT, TN_OUT), lambda i, j: (i, j)),
        out_shape=jax.ShapeDtypeStruct((m, D_MODEL), F32),
        scratch_shapes=[pltpu.VMEM((TM_OUT, D_MODEL), BF16)],
        compiler_params=_params("parallel", "arbitrary"),
        name="outproj",
    )(oa, ob, oc, od, gg, w, x2)


def _ffn_kernel(x_ref, g_ref, wg_ref, wu_ref, wd_ref, o_ref, h_ref, acc_ref):
    f = pl.program_id(1)

    @pl.when(f == 0)
    def _():
        x = x_ref[...]
        ms = jnp.mean(x * x, axis=-1, keepdims=True)
        h_ref[...] = ((x * lax.rsqrt(ms + 1e-6)) * g_ref[...]).astype(BF16)
        acc_ref[...] = jnp.zeros_like(acc_ref)

    h = h_ref[...]
    gate = _dot(h, wg_ref[...])
    up = _dot(h, wu_ref[...])
    act = (gate * jax.nn.sigmoid(gate)) * up
    acc_ref[...] += _dot(act.astype(BF16), wd_ref[...])

    @pl.when(f == pl.num_programs(1) - 1)
    def _():
        o_ref[...] = x_ref[...] + acc_ref[...]


def _ffn(x2, gain, wg, wu, wd):
    m = x2.shape[0]
    return pl.pallas_call(
        _ffn_kernel,
        grid=(m // TM_FFN, D_FF // TF_FFN),
        in_specs=[pl.BlockSpec((TM_FFN, D_MODEL), lambda i, f: (i, 0)),
                  pl.BlockSpec((1, D_MODEL), lambda i, f: (0, 0)),
                  pl.BlockSpec((D_MODEL, TF_FFN), lambda i, f: (0, f)),
                  pl.BlockSpec((D_MODEL, TF_FFN), lambda i, f: (0, f)),
                  pl.BlockSpec((TF_FFN, D_MODEL), lambda i, f: (f, 0))],
        out_specs=pl.BlockSpec((TM_FFN, D_MODEL), lambda i, f: (i, 0)),
        out_shape=jax.ShapeDtypeStruct((m, D_MODEL), F32),
        scratch_shapes=[pltpu.VMEM((TM_FFN, D_MODEL), BF16), pltpu.VMEM((TM_FFN, D_MODEL), F32)],
        compiler_params=_params("parallel", "arbitrary"),
        name="ffn",
    )(x2, gain, wg, wu, wd)


def _static_tables():
    i = np.arange(TQ)[:, None]
    j = np.arange(TK)[None, :]
    delta = np.arange(SEQ // TK)[:, None, None]
    bkt_t = _rel_bucket_np(delta * TK + i[None] - j[None]).reshape(-1, TK)
    t = np.arange(SEQ)[:, None]
    n = np.arange(LANES)[None, :]
    bkt_c = _rel_bucket_np(t - (n * CMP_STRIDE + CMP_LEN - 1))
    c0 = np.arange(LANES)[:, None] * CMP_STRIDE
    s0 = np.arange(LANES)[None, :] * SLC_LEN
    ov = np.minimum(c0 + CMP_LEN, s0 + SLC_LEN) - np.maximum(c0, s0)
    ov = np.maximum(ov, 0) / CMP_LEN
    ov[N_CMP:, :] = 0.0
    ov[:, N_SLC:] = 0.0
    tri = (np.arange(TK)[:, None] > np.arange(TK)[None, :]).astype(np.float32)
    return bkt_t, bkt_c, ov.astype(np.float32), tri


def _permute_w_in(w_in):
    depth = w_in.shape[0]
    pad = jnp.zeros((depth, D_MODEL, PROJ_COLS - W_IN_COLS), w_in.dtype)
    gate_pad, tail_pad = pad[..., :LANES - 3 * N_HEADS], pad[..., LANES - 3 * N_HEADS:]
    return jnp.concatenate([w_in[..., 0:512], w_in[..., 1304:5400], w_in[..., 512:1280],
                            w_in[..., 1280:1304], gate_pad, tail_pad], axis=-1).astype(BF16)


def kernel(x, w_in, w_out, norm_mix, norm_ffn, q_gain, k_gain, cmp_pos, cmp_w1, cmp_w2, rel_table, conv_w,
           sgu_w, sgu_b, group_gain, w_ffn_gate, w_ffn_up, w_ffn_down):
    batch, seq, d_model = x.shape
    assert seq == SEQ and d_model == D_MODEL
    depth = w_in.shape[0]
    m = batch * seq
    nhb = SEQ // CMP_STRIDE

    bkt_t, bkt_c, ov_np, tri_np = _static_tables()
    bias_t = _bias_expand(rel_table, bkt_t)
    bias_t = bias_t.reshape(N_HEADS, SEQ // TK, TQ, TK).transpose(1, 0, 2, 3)
    bias_c = _bias_expand(rel_table, bkt_c)
    overlap = jnp.asarray(ov_np, BF16)
    tri = jnp.asarray(tri_np, BF16)

    w_in_p = _permute_w_in(w_in)
    w_out_b = w_out.astype(BF16)
    wg_b = w_ffn_gate.astype(BF16)
    wu_b = w_ffn_up.astype(BF16)
    wd_b = w_ffn_down.astype(BF16)
    w1_b = cmp_w1.astype(BF16)
    w2_b = cmp_w2.astype(BF16)
    pos = cmp_pos.reshape(depth, 2, 1, CMP_LEN * HEAD_DIM)
    kgain2 = jnp.tile(k_gain, (1, 2))
    sgu_bx = jnp.repeat(jnp.swapaxes(sgu_b, 1, 2), HEAD_DIM, axis=2)

    x2 = x.reshape(m, D_MODEL)
    for l in range(depth):
        proj = _inproj(x2, norm_mix[l][None], w_in_p[l])
        hb = proj[:, COL_KV:COL_KV + 2 * LANES].astype(BF16)
        hb = hb.reshape(batch, nhb, CMP_STRIDE, 2, NSA_KV_HEADS, HEAD_DIM).transpose(0, 3, 4, 1, 2, 5)
        hb = hb.reshape(batch, 2, NSA_KV_HEADS, nhb, CMP_STRIDE * HEAD_DIM)
        kc, vc, ks, vs, kw, vw = _nsa_prep(proj, hb, w1_b[l], w2_b[l], pos[l], k_gain[l][None],
                                           kgain2[l][None], batch)
        o_a = _nsa_attention(proj, kc, vc, ks, vs, kw, vw, bias_c, bias_t, overlap, q_gain[l][None], batch)
        o_d = _stickbreak(proj, tri, batch)
        o_b, o_c = _conv_sgu(proj, conv_w[l], sgu_w[l], sgu_bx[l])
        x2 = _outproj(o_a, o_b, o_c, o_d, group_gain[l].reshape(4, GROUP_WIDTH), w_out_b[l], x2)
        x2 = _ffn(x2, norm_ffn[l][None], wg_b[l], wu_b[l], wd_b[l])
    return x2.reshape(batch, seq, d_model)
```

```python
import math

import numpy as np
import jax
import jax.numpy as jnp
from jax import lax
from jax.experimental import pallas as pl
from jax.experimental.pallas import tpu as pltpu

F32 = jnp.float32
BF16 = jnp.bfloat16

D_MODEL = 2048
SEQ = 2048
HEAD_DIM = 64
GROUP_WIDTH = 512
N_HEADS = 8
NSA_KV_HEADS = 2
NSA_GQA = 4
CMP_LEN = 32
CMP_STRIDE = 16
SLC_LEN = 64
SLC_TOP = 16
WINDOW = 512
FORCE_BONUS = 1.0e3
N_BUCKETS = 32
MAX_DISTANCE = 1024
CONV_W = 3
CHUNK = 128
D_FF = 5632
NEG_INF = -1.0e30
N_CMP = (SEQ - CMP_LEN) // CMP_STRIDE + 1
N_SLC = SEQ // SLC_LEN
W_IN_COLS = 5400

LANES = 128
VMEM_LIMIT = 56 * 1024 * 1024

COL_Q = 0
COL_BG, COL_BC, COL_BH = 512, 1024, 1536
COL_CU, COL_CV = 2048, 2560
COL_DQ, COL_DK, COL_DV = 3072, 3584, 4096
COL_KV = 4608
COL_GATE = 5376
PROJ_COLS = 5632

TM_IN, TN_IN = 512, 1408
TQ = 128
TK = 128
N_KT = SEQ // TK
WIN_TILES = WINDOW // TK + 1
TQ_SB = 256
TM_BC = 256
TM_OUT, TN_OUT = 512, 1024
TM_FFN, TF_FFN = 512, 512
MASKED_BUCKET = N_BUCKETS


def _params(*sem):
    return pltpu.CompilerParams(dimension_semantics=sem, vmem_limit_bytes=VMEM_LIMIT)


def _dot(a, b):
    return jnp.dot(a, b, preferred_element_type=F32)


def _dot_nt(a, b):
    return lax.dot_general(a, b, (((1,), (1,)), ((), ())), preferred_element_type=F32)


def _split_dot(a, b, terms):
    out = None
    rem = a
    for i in range(terms):
        part = rem.astype(BF16)
        d = _dot(part, b)
        out = d if out is None else out + d
        if i + 1 < terms:
            rem = rem - part.astype(F32)
    return out


def _gelu(x):
    c = math.sqrt(2.0 / math.pi)
    return x * (0.5 * (1.0 + jnp.tanh(c * (x + 0.044715 * (x * x * x)))))


def _rel_bucket_np(dist):
    n = np.maximum(dist, 0)
    max_exact = N_BUCKETS // 2
    nf = np.maximum(n, 1).astype(np.float64)
    large = max_exact + (np.log(nf / max_exact) / math.log(MAX_DISTANCE / max_exact)
                         * (N_BUCKETS - max_exact)).astype(np.int32)
    large = np.minimum(large, N_BUCKETS - 1)
    return np.where(n < max_exact, n, large).astype(np.int32)


def _bias_expand_kernel(tab_ref, bkt_ref, o_ref):
    bkt = bkt_ref[...]
    for h in range(N_HEADS):
        acc = jnp.full(bkt.shape, NEG_INF, F32)
        for b in range(N_BUCKETS):
            acc = jnp.where(bkt == b, tab_ref[b, h], acc)
        o_ref[h] = acc


def _bias_expand(rel_table, bucket_np):
    rows = bucket_np.shape[0]
    tr = 128
    return pl.pallas_call(
        _bias_expand_kernel,
        grid=(rows // tr,),
        in_specs=[pl.BlockSpec(memory_space=pltpu.SMEM),
                  pl.BlockSpec((tr, LANES), lambda i: (i, 0))],
        out_specs=pl.BlockSpec((N_HEADS, tr, LANES), lambda i: (0, i, 0)),
        out_shape=jax.ShapeDtypeStruct((N_HEADS, rows, LANES), F32),
        compiler_params=_params("parallel"),
        name="bias_expand",
    )(rel_table, jnp.asarray(bucket_np))


def _inproj_kernel(x_ref, g_ref, w_ref, o_ref, h_ref):
    @pl.when(pl.program_id(1) == 0)
    def _():
        x = x_ref[...]
        ms = jnp.mean(x * x, axis=-1, keepdims=True)
        h_ref[...] = ((x * lax.rsqrt(ms + 1e-6)) * g_ref[...]).astype(BF16)

    o_ref[...] = _dot(h_ref[...], w_ref[...])


def _inproj(x2, gain, w):
    m = x2.shape[0]
    return pl.pallas_call(
        _inproj_kernel,
        grid=(m // TM_IN, PROJ_COLS // TN_IN),
        in_specs=[pl.BlockSpec((TM_IN, D_MODEL), lambda i, j: (i, 0)),
                  pl.BlockSpec((1, D_MODEL), lambda i, j: (0, 0)),
                  pl.BlockSpec((D_MODEL, TN_IN), lambda i, j: (0, j))],
        out_specs=pl.BlockSpec((TM_IN, TN_IN), lambda i, j: (i, j)),
        out_shape=jax.ShapeDtypeStruct((m, PROJ_COLS), F32),
        scratch_shapes=[pltpu.VMEM((TM_IN, D_MODEL), BF16)],
        compiler_params=_params("parallel", "arbitrary"),
        name="inproj",
    )(x2, gain, w)


def _seg_rms(x, gain2):
    lane = lax.broadcasted_iota(jnp.int32, x.shape, 1)
    x2 = x * x
    lo = lane < HEAD_DIM
    s0 = jnp.sum(jnp.where(lo, x2, 0.0), axis=-1, keepdims=True)
    s1 = jnp.sum(jnp.where(lo, 0.0, x2), axis=-1, keepdims=True)
    ms = jnp.where(lo, s0, s1) * (1.0 / HEAD_DIM)
    return (x * lax.rsqrt(ms + 1e-6)) * gain2


def _nsa_prep_kernel(kv_ref, hb_ref, w1_ref, w2_ref, pos_ref, kg_ref, kg2_ref,
                     kc_ref, vc_ref, ks_ref, vs_ref, kw_ref, vw_ref):
    kg2 = kg2_ref[...]
    ks_ref[0] = _seg_rms(kv_ref[:, 2 * LANES:3 * LANES], kg2).astype(BF16)
    vs_ref[0] = kv_ref[:, 3 * LANES:4 * LANES].astype(BF16)
    kw_ref[0] = _seg_rms(kv_ref[:, 4 * LANES:5 * LANES], kg2).astype(BF16)
    vw_ref[0] = kv_ref[:, 5 * LANES:6 * LANES].astype(BF16)

    half = CMP_STRIDE * HEAD_DIM
    for i, out_ref in ((0, kc_ref), (1, vc_ref)):
        w1 = w1_ref[i]
        pos = jnp.broadcast_to(pos_ref[i], (8, CMP_LEN * HEAD_DIM)).astype(BF16)
        posterm = _dot(pos, w1)[0:1]
        for g in range(NSA_KV_HEADS):
            hb = hb_ref[0, i, g]
            p1 = _dot(hb, w1[:half])
            p2 = _dot(hb, w1[half:])
            pre = p1 + pltpu.roll(p2, SEQ // CMP_STRIDE - 1, axis=0) + posterm
            out = _dot(_gelu(pre).astype(BF16), w2_ref[i])
            if i == 0:
                ms = jnp.mean(out * out, axis=-1, keepdims=True)
                out = (out * lax.rsqrt(ms + 1e-6)) * kg_ref[...]
            out_ref[0, g] = out.astype(BF16)


def _nsa_prep(proj, hb, w1, w2, pos, kgain, kgain2, batch):
    nhb = SEQ // CMP_STRIDE
    cmp_shape = jax.ShapeDtypeStruct((batch, NSA_KV_HEADS, nhb, HEAD_DIM), BF16)
    tok_shape = jax.ShapeDtypeStruct((batch, SEQ, LANES), BF16)
    cmp_spec = pl.BlockSpec((1, NSA_KV_HEADS, nhb, HEAD_DIM), lambda b: (b, 0, 0, 0))
    tok_spec = pl.BlockSpec((1, SEQ, LANES), lambda b: (b, 0, 0))
    return pl.pallas_call(
        _nsa_prep_kernel,
        grid=(batch,),
        in_specs=[pl.BlockSpec((SEQ, 6 * LANES), lambda b: (b, COL_KV // (6 * LANES))),
                  pl.BlockSpec((1, 2, NSA_KV_HEADS, nhb, CMP_STRIDE * HEAD_DIM), lambda b: (b, 0, 0, 0, 0)),
                  pl.BlockSpec((2, CMP_LEN * HEAD_DIM, HEAD_DIM), lambda b: (0, 0, 0)),
                  pl.BlockSpec((2, HEAD_DIM, HEAD_DIM), lambda b: (0, 0, 0)),
                  pl.BlockSpec((2, 1, CMP_LEN * HEAD_DIM), lambda b: (0, 0, 0)),
                  pl.BlockSpec((1, HEAD_DIM), lambda b: (0, 0)),
                  pl.BlockSpec((1, LANES), lambda b: (0, 0))],
        out_specs=[cmp_spec, cmp_spec, tok_spec, tok_spec, tok_spec, tok_spec],
        out_shape=[cmp_shape, cmp_shape, tok_shape, tok_shape, tok_shape, tok_shape],
        compiler_params=_params("parallel"),
        name="nsa_prep",
    )(proj, hb, w1, w2, pos, kgain, kgain2)


def _nsa_kernel(q_ref, gate_ref, kc_ref, vc_ref, kst_ref, vsx_ref, kwt_ref, vwx_ref,
                bc_ref, bt_ref, bw_ref, ov_ref, ex_ref, qg_ref, o_ref):
    qi = pl.program_id(1)
    rows = NSA_GQA * TQ
    q = q_ref[...]
    qg = qg_ref[...] * (HEAD_DIM ** -0.5)
    qn = []
    for h in range(N_HEADS):
        seg = q[:, h * HEAD_DIM:(h + 1) * HEAD_DIM]
        ms = jnp.mean(seg * seg, axis=-1, keepdims=True)
        qn.append(((seg * lax.rsqrt(ms + 1e-6)) * qg).astype(BF16))
    qst = [jnp.concatenate(qn[g * NSA_GQA:(g + 1) * NSA_GQA], axis=0) for g in range(NSA_KV_HEADS)]
    gates = jax.nn.sigmoid(gate_ref[...])

    t1 = qi * TQ + lax.broadcasted_iota(jnp.int32, (TQ, LANES), 0)
    t4 = jnp.concatenate([t1] * NSA_GQA, axis=0)
    lane1 = lax.broadcasted_iota(jnp.int32, (TQ, LANES), 1)
    lane4 = lax.broadcasted_iota(jnp.int32, (rows, LANES), 1)
    blk_t = lax.broadcasted_iota(jnp.int32, (N_SLC, TQ), 0)

    o_cmp, sel = [], []
    for g in range(NSA_KV_HEADS):
        s = _dot_nt(qst[g], kc_ref[0, g]) + bc_ref[g * NSA_GQA:(g + 1) * NSA_GQA].reshape(rows, LANES)
        mask_c = (t4 - (lane4 * CMP_STRIDE + (CMP_LEN - 1)) >= 0) & (lane4 < N_CMP)
        s = jnp.where(mask_c, s, NEG_INF)
        m = jnp.max(s, axis=-1, keepdims=True)
        e = jnp.where(mask_c, jnp.exp(s - m), 0.0)
        l = jnp.sum(e, axis=-1, keepdims=True)
        p = e * jnp.where(l > 0.0, 1.0 / l, 0.0)
        o_cmp.append(_dot(p.astype(BF16), vc_ref[0, g]))

        psum = p[0:TQ] + p[TQ:2 * TQ] + p[2 * TQ:3 * TQ] + p[3 * TQ:4 * TQ]
        imp = _split_dot(psum, ov_ref[...], 3)
        cur = t1 >> 6
        valid = (lane1 * SLC_LEN <= t1) & (lane1 < N_SLC)
        forced = (lane1 == 0) | (lane1 == cur) | (lane1 == cur - 1)
        score = jnp.where(valid, imp + jnp.where(forced, FORCE_BONUS, 0.0), NEG_INF)
        st = score.T[0:N_SLC]
        rank = jnp.zeros((N_SLC, TQ), F32)
        for k in range(N_SLC):
            ck = st[k:k + 1, :]
            before = (ck > st) | ((ck == st) & (blk_t > k))
            rank = rank + jnp.where(before, 1.0, 0.0)
        sel_t = jnp.where(rank < float(SLC_TOP), 1.0, 0.0)
        sel_t = jnp.concatenate([sel_t, jnp.zeros((LANES - N_SLC, TQ), F32)], axis=0)
        sel.append(sel_t.T.astype(BF16))

    def flash_step(carry, qk, bias, vx):
        m_i, acc = carry
        sc = qk + bias
        m_new = jnp.maximum(m_i, jnp.max(sc, axis=-1, keepdims=True))
        alpha = jnp.exp(m_i - m_new)
        pe = jnp.exp(sc - m_new)
        return m_new, alpha * acc + _dot(pe.astype(BF16), vx)

    def slc_logits(kj, g):
        unsel = (_dot(sel[g], ex_ref[kj]) - 1.0) * (-NEG_INF)
        return _dot(qst[g], kst_ref[0, g, kj]), unsel

    def slc_update(kj, g, carry, logits):
        qk, unsel = logits
        off = pl.multiple_of(kj * TK, TK)
        bias = bt_ref[qi - kj, g * NSA_GQA:(g + 1) * NSA_GQA]
        bias = (bias + unsel[None]).reshape(rows, TK)
        return flash_step(carry, qk, bias, vsx_ref[0, g, pl.ds(off, TK), :])

    def win_update(kj, g, carry, qk):
        off = pl.multiple_of(kj * TK, TK)
        bias = bw_ref[qi - kj, g * NSA_GQA:(g + 1) * NSA_GQA].reshape(rows, TK)
        return flash_step(carry, qk, bias, vwx_ref[0, g, pl.ds(off, TK), :])

    init1 = (jnp.full((rows, 1), NEG_INF, F32), jnp.zeros((rows, LANES), F32))
    lo = jnp.maximum(qi - (WIN_TILES - 1), 0)
    groups = range(NSA_KV_HEADS)

    def far_body(kj, carry):
        logits = [slc_logits(kj, g) for g in groups]
        return tuple(slc_update(kj, g, carry[g], logits[g]) for g in groups)

    slc_c = lax.fori_loop(0, lo, far_body, (init1,) * NSA_KV_HEADS)

    def near_body(kj, carry):
        s_c, w_c = carry
        s_logits = [slc_logits(kj, g) for g in groups]
        w_logits = [_dot(qst[g], kwt_ref[0, g, kj]) for g in groups]
        s_c = tuple(slc_update(kj, g, s_c[g], s_logits[g]) for g in groups)
        w_c = tuple(win_update(kj, g, w_c[g], w_logits[g]) for g in groups)
        return s_c, w_c

    slc_c, win_c = lax.fori_loop(lo, qi + 1, near_body, (slc_c, (init1,) * NSA_KV_HEADS))

    outs = []
    for g in range(NSA_KV_HEADS):
        acc_s, acc_w = slc_c[g][1], win_c[g][1]
        o_slc = acc_s[:, :HEAD_DIM] / acc_s[:, HEAD_DIM:HEAD_DIM + 1]
        o_win = acc_w[:, :HEAD_DIM] / acc_w[:, HEAD_DIM:HEAD_DIM + 1]
        for r in range(NSA_GQA):
            h = g * NSA_GQA + r
            rs = slice(r * TQ, (r + 1) * TQ)
            outs.append(gates[:, 3 * h:3 * h + 1] * o_cmp[g][rs]
                        + gates[:, 3 * h + 1:3 * h + 2] * o_slc[rs]
                        + gates[:, 3 * h + 2:3 * h + 3] * o_win[rs])
    o_ref[...] = jnp.concatenate(outs, axis=-1)


def _nsa_attention(proj, kc, vc, kst, vsx, kwt, vwx, bias_c, bias_t, bias_w, overlap, expand, qgain, batch):
    nq = SEQ // TQ
    nhb = SEQ // CMP_STRIDE
    cmp_spec = pl.BlockSpec((1, NSA_KV_HEADS, nhb, HEAD_DIM), lambda b, i: (b, 0, 0, 0))
    kt_spec = pl.BlockSpec((1, NSA_KV_HEADS, N_KT, HEAD_DIM, TK), lambda b, i: (b, 0, 0, 0, 0))
    vx_spec = pl.BlockSpec((1, NSA_KV_HEADS, SEQ, LANES), lambda b, i: (b, 0, 0, 0))
    return pl.pallas_call(
        _nsa_kernel,
        grid=(batch, nq),
        in_specs=[pl.BlockSpec((TQ, GROUP_WIDTH), lambda b, i: (b * nq + i, COL_Q // GROUP_WIDTH)),
                  pl.BlockSpec((TQ, LANES), lambda b, i: (b * nq + i, COL_GATE // LANES)),
                  cmp_spec, cmp_spec, kt_spec, vx_spec, kt_spec, vx_spec,
                  pl.BlockSpec((N_HEADS, TQ, LANES), lambda b, i: (0, i, 0)),
                  pl.BlockSpec((N_KT, N_HEADS, TQ, TK), lambda b, i: (0, 0, 0, 0)),
                  pl.BlockSpec((WIN_TILES, N_HEADS, TQ, TK), lambda b, i: (0, 0, 0, 0)),
                  pl.BlockSpec((LANES, LANES), lambda b, i: (0, 0)),
                  pl.BlockSpec((N_KT, LANES, TK), lambda b, i: (0, 0, 0)),
                  pl.BlockSpec((1, HEAD_DIM), lambda b, i: (0, 0))],
        out_specs=pl.BlockSpec((TQ, GROUP_WIDTH), lambda b, i: (b * nq + i, 0)),
        out_shape=jax.ShapeDtypeStruct((batch * SEQ, GROUP_WIDTH), F32),
        compiler_params=_params("parallel", "arbitrary"),
        name="nsa_attention",
    )(proj, proj, kc, vc, kst, vsx, kwt, vwx, bias_c, bias_t, bias_w, overlap, expand, qgain)


def _stickbreak_kernel(q_ref, kt_ref, v_ref, tri_ref, o_ref, *scratch):
    acc_ref, tc_ref = scratch[:N_HEADS], scratch[N_HEADS:]
    qi = pl.program_id(1)
    q = q_ref[...] * (HEAD_DIM ** -0.5)
    qh = [q[:, h * HEAD_DIM:(h + 1) * HEAD_DIM].astype(BF16) for h in range(N_HEADS)]
    t = qi * TQ_SB + lax.broadcasted_iota(jnp.int32, (TQ_SB, TK), 0)
    lane = lax.broadcasted_iota(jnp.int32, (TQ_SB, TK), 1)
    tri = tri_ref[...]
    for h in range(N_HEADS):
        acc_ref[h][...] = jnp.zeros((TQ_SB, LANES), F32)
        tc_ref[h][...] = jnp.zeros((TQ_SB, TK), F32)

    def tile_step(kj, masked):
        off = pl.multiple_of(kj * TK, TK)
        heads = range(N_HEADS)
        if masked:
            ok = (lane + off) < t
        z = [_dot(qh[h], kt_ref[0, kj, h * HEAD_DIM:(h + 1) * HEAD_DIM, :]) for h in heads]
        log_beta, r = [], []
        for h in heads:
            lb = jnp.minimum(z[h], 0.0) - jnp.log1p(jnp.exp(-jnp.abs(z[h])))
            log_1m = lb - z[h]
            if masked:
                log_1m = jnp.where(ok, log_1m, 0.0)
            log_beta.append(lb)
            r.append(_split_dot(log_1m, tri, 2))
        for h in heads:
            vt = v_ref[pl.ds(off, TK), (h // 2) * LANES:(h // 2 + 1) * LANES]
            a = jnp.exp(log_beta[h] + r[h][:, :TK] + tc_ref[h][...])
            if masked:
                a = jnp.where(ok, a, 0.0)
            acc_ref[h][...] += _dot(a.astype(BF16), vt)
            tc_ref[h][...] += r[h][:, TK:]

    n_diag = TQ_SB // TK
    for d in range(n_diag):
        tile_step((qi + 1) * n_diag - 1 - d, True)

    def body(i, carry):
        tile_step(qi * n_diag - 1 - i, False)
        return carry

    lax.fori_loop(0, qi * n_diag, body, 0)

    lane_o = lax.broadcasted_iota(jnp.int32, (TQ_SB, LANES), 1)
    for pair in range(N_HEADS // 2):
        o_ref[:, pair * LANES:(pair + 1) * LANES] = jnp.where(
            lane_o < HEAD_DIM, acc_ref[2 * pair][...], acc_ref[2 * pair + 1][...])


def _stickbreak(proj, kt, v, tri, batch):
    nq = SEQ // TQ_SB
    return pl.pallas_call(
        _stickbreak_kernel,
        grid=(batch, nq),
        in_specs=[pl.BlockSpec((TQ_SB, GROUP_WIDTH), lambda b, i: (b * nq + i, COL_DQ // GROUP_WIDTH)),
                  pl.BlockSpec((1, N_KT, GROUP_WIDTH, TK), lambda b, i: (b, 0, 0, 0)),
                  pl.BlockSpec((SEQ, GROUP_WIDTH), lambda b, i: (b, 0)),
                  pl.BlockSpec((TK, 2 * TK), lambda b, i: (0, 0))],
        out_specs=pl.BlockSpec((TQ_SB, GROUP_WIDTH), lambda b, i: (b * nq + i, 0)),
        out_shape=jax.ShapeDtypeStruct((batch * SEQ, GROUP_WIDTH), F32),
        scratch_shapes=[pltpu.VMEM((TQ_SB, LANES), F32)] * (2 * N_HEADS),
        compiler_params=_params("parallel", "arbitrary"),
        name="stickbreak",
    )(proj, kt, v, tri)


def _conv_sgu_kernel(bg_ref, cg_ref, hh_ref, cgp_ref, hhp_ref, cu_ref, cv_ref, cw_ref, sw_ref, sb_ref,
                     ob_ref, oc_ref):
    i = pl.program_id(0)
    z = cg_ref[...] * hh_ref[...]
    first = (i % (SEQ // TM_BC)) == 0
    zp = jnp.where(first, 0.0, cgp_ref[...] * hhp_ref[...])
    row = lax.broadcasted_iota(jnp.int32, z.shape, 0)
    p1 = jnp.broadcast_to(zp[7:8], z.shape)
    p2 = jnp.broadcast_to(zp[6:7], z.shape)
    z1 = jnp.where(row == 0, p1, pltpu.roll(z, 1, axis=0))
    z2 = jnp.where(row == 0, p2, jnp.where(row == 1, p1, pltpu.roll(z, 2, axis=0)))
    cw = cw_ref[...]
    y = cw[0:1] * z2
    y = y + cw[1:2] * z1
    y = y + cw[2:3] * z
    ob_ref[...] = bg_ref[...] * y

    u = _gelu(cu_ref[...])
    v = _gelu(cv_ref[...])
    mu = jnp.mean(v, axis=-1, keepdims=True)
    vc = v - mu
    var = jnp.mean(vc * vc, axis=-1, keepdims=True)
    vn = (vc * lax.rsqrt(var + 1e-5)).astype(BF16)
    pr = lax.broadcasted_iota(jnp.int32, (CHUNK, CHUNK), 0)
    pc = lax.broadcasted_iota(jnp.int32, (CHUNK, CHUNK), 1)
    ws = [jnp.where(pr >= pc, sw_ref[h], 0.0).astype(BF16) for h in range(N_HEADS)]
    for c in range(TM_BC // CHUNK):
        rs = slice(c * CHUNK, (c + 1) * CHUNK)
        s = jnp.concatenate(
            [_dot(ws[h], vn[rs, h * HEAD_DIM:(h + 1) * HEAD_DIM]) for h in range(N_HEADS)], axis=-1)
        oc_ref[rs, :] = u[rs] * (s + sb_ref[...])


def _conv_sgu(proj, conv_w, sgu_w, sgu_bx):
    m = proj.shape[0]
    gw = GROUP_WIDTH

    def col(c):
        return pl.BlockSpec((TM_BC, gw), lambda i: (i, c // gw))

    def prev(c):
        return pl.BlockSpec((8, gw), lambda i: (jnp.maximum(i * (TM_BC // 8) - 1, 0), c // gw))

    out = jax.ShapeDtypeStruct((m, gw), F32)
    return pl.pallas_call(
        _conv_sgu_kernel,
        grid=(m // TM_BC,),
        in_specs=[col(COL_BG), col(COL_BC), col(COL_BH), prev(COL_BC), prev(COL_BH), col(COL_CU), col(COL_CV),
                  pl.BlockSpec((CONV_W, gw), lambda i: (0, 0)),
                  pl.BlockSpec((N_HEADS, CHUNK, CHUNK), lambda i: (0, 0, 0)),
                  pl.BlockSpec((CHUNK, gw), lambda i: (0, 0))],
        out_specs=[pl.BlockSpec((TM_BC, gw), lambda i: (i, 0))] * 2,
        out_shape=[out, out],
        compiler_params=_params("parallel"),
        name="conv_sgu",
    )(proj, proj, proj, proj, proj, proj, proj, conv_w, sgu_w, sgu_bx)


def _outproj_kernel(oa_ref, ob_ref, oc_ref, od_ref, gg_ref, w_ref, x_ref, o_ref, mix_ref):
    @pl.when(pl.program_id(1) == 0)
    def _():
        for i, ref in enumerate((oa_ref, ob_ref, oc_ref, od_ref)):
            o = ref[...]
            ms = jnp.mean(o * o, axis=-1, keepdims=True)
            mix_ref[:, i * GROUP_WIDTH:(i + 1) * GROUP_WIDTH] = (
                (o * lax.rsqrt(ms + 1e-6)) * gg_ref[i:i + 1]).astype(BF16)

    o_ref[...] = x_ref[...] + _dot(mix_ref[...], w_ref[...])


def _outproj(oa, ob, oc, od, gg, w, x2):
    m = x2.shape[0]
    mix_spec = pl.BlockSpec((TM_OUT, GROUP_WIDTH), lambda i, j: (i, 0))
    return pl.pallas_call(
        _outproj_kernel,
        grid=(m // TM_OUT, D_MODEL // TN_OUT),
        in_specs=[mix_spec, mix_spec, mix_spec, mix_spec,
                  pl.BlockSpec((4, GROUP_WIDTH), lambda i, j: (0, 0)),
                  pl.BlockSpec((D_MODEL, TN_OUT), lambda i, j: (0, j)),
                  pl.BlockSpec((TM_OUT, TN_OUT), lambda i, j: (i, j))],
        out_specs=pl.BlockSpec((TM_OUT, TN_OUT), lambda i, j: (i, j)),
        out_shape=jax.ShapeDtypeStruct((m, D_MODEL), F32),
        scratch_shapes=[pltpu.VMEM((TM_OUT, D_MODEL), BF16)],
        compiler_params=_params("parallel", "arbitrary"),
        name="outproj",
    )(oa, ob, oc, od, gg, w, x2)


def _ffn_kernel(x_ref, g_ref, wg_ref, wu_ref, wd_ref, o_ref, h_ref, acc_ref):
    f = pl.program_id(1)

    @pl.when(f == 0)
    def _():
        x = x_ref[...]
        ms = jnp.mean(x * x, axis=-1, keepdims=True)
        h_ref[...] = ((x * lax.rsqrt(ms + 1e-6)) * g_ref[...]).astype(BF16)
        acc_ref[...] = jnp.zeros_like(acc_ref)

    h = h_ref[...]
    gate = _dot(h, wg_ref[...])
    up = _dot(h, wu_ref[...])
    act = (gate * jax.nn.sigmoid(gate)) * up
    acc_ref[...] += _dot(act.astype(BF16), wd_ref[...])

    @pl.when(f == pl.num_programs(1) - 1)
    def _():
        o_ref[...] = x_ref[...] + acc_ref[...]


def _ffn(x2, gain, wg, wu, wd):
    m = x2.shape[0]
    return pl.pallas_call(
        _ffn_kernel,
        grid=(m // TM_FFN, D_FF // TF_FFN),
        in_specs=[pl.BlockSpec((TM_FFN, D_MODEL), lambda i, f: (i, 0)),
                  pl.BlockSpec((1, D_MODEL), lambda i, f: (0, 0)),
                  pl.BlockSpec((D_MODEL, TF_FFN), lambda i, f: (0, f)),
                  pl.BlockSpec((D_MODEL, TF_FFN), lambda i, f: (0, f)),
                  pl.BlockSpec((TF_FFN, D_MODEL), lambda i, f: (f, 0))],
        out_specs=pl.BlockSpec((TM_FFN, D_MODEL), lambda i, f: (i, 0)),
        out_shape=jax.ShapeDtypeStruct((m, D_MODEL), F32),
        scratch_shapes=[pltpu.VMEM((TM_FFN, D_MODEL), BF16), pltpu.VMEM((TM_FFN, D_MODEL), F32)],
        compiler_params=_params("parallel", "arbitrary"),
        name="ffn",
    )(x2, gain, wg, wu, wd)


def _static_tables():
    i = np.arange(TQ)[:, None]
    j = np.arange(TK)[None, :]
    dist_t = np.arange(N_KT)[:, None, None] * TK + i[None] - j[None]
    bkt_t = np.where(dist_t >= 0, _rel_bucket_np(dist_t), MASKED_BUCKET)
    dist_w = dist_t[:WIN_TILES]
    bkt_w = np.where((dist_w >= 0) & (dist_w < WINDOW), _rel_bucket_np(dist_w), MASKED_BUCKET)
    t = np.arange(SEQ)[:, None]
    n = np.arange(LANES)[None, :]
    bkt_c = _rel_bucket_np(t - (n * CMP_STRIDE + CMP_LEN - 1))
    c0 = np.arange(LANES)[:, None] * CMP_STRIDE
    s0 = np.arange(LANES)[None, :] * SLC_LEN
    ov = np.minimum(c0 + CMP_LEN, s0 + SLC_LEN) - np.maximum(c0, s0)
    ov = np.maximum(ov, 0) / CMP_LEN
    ov[N_CMP:, :] = 0.0
    ov[:, N_SLC:] = 0.0
    key_blk = (np.arange(N_KT)[:, None] * TK + np.arange(TK)[None, :]) // SLC_LEN
    expand = (np.arange(LANES)[None, :, None] == key_blk[:, None, :]).astype(np.float32)
    tri = (np.arange(TK)[:, None] > np.arange(TK)[None, :]).astype(np.float32)
    tri = np.concatenate([tri, np.ones((TK, TK), np.float32)], axis=1)
    return (bkt_t.reshape(-1, TK).astype(np.int32), bkt_w.reshape(-1, TK).astype(np.int32), bkt_c,
            ov.astype(np.float32), expand, tri)


def _permute_w_in(w_in):
    depth = w_in.shape[0]
    pad = jnp.zeros((depth, D_MODEL, PROJ_COLS - W_IN_COLS), w_in.dtype)
    gate_pad, tail_pad = pad[..., :LANES - 3 * N_HEADS], pad[..., LANES - 3 * N_HEADS:]
    return jnp.concatenate([w_in[..., 0:512], w_in[..., 1304:5400], w_in[..., 512:1280],
                            w_in[..., 1280:1304], gate_pad, tail_pad], axis=-1).astype(BF16)


def _key_tiles(k, batch):
    k = k.reshape(batch, N_KT, TK, NSA_KV_HEADS, HEAD_DIM)
    return k.transpose(0, 3, 1, 4, 2)


def _values_ones(v, batch):
    v = v.reshape(batch, SEQ, NSA_KV_HEADS, HEAD_DIM).transpose(0, 2, 1, 3)
    return jnp.concatenate([v, jnp.ones_like(v)], axis=-1)


def kernel(x, w_in, w_out, norm_mix, norm_ffn, q_gain, k_gain, cmp_pos, cmp_w1, cmp_w2, rel_table, conv_w,
           sgu_w, sgu_b, group_gain, w_ffn_gate, w_ffn_up, w_ffn_down):
    batch, seq, d_model = x.shape
    assert seq == SEQ and d_model == D_MODEL
    depth = w_in.shape[0]
    m = batch * seq
    nhb = SEQ // CMP_STRIDE

    bkt_t, bkt_w, bkt_c, ov_np, expand_np, tri_np = _static_tables()

    def tiles(b):
        return b.reshape(N_HEADS, -1, TQ, TK).transpose(1, 0, 2, 3)

    bias_t = tiles(_bias_expand(rel_table, bkt_t))
    bias_w = tiles(_bias_expand(rel_table, bkt_w))
    bias_c = _bias_expand(rel_table, bkt_c)
    overlap = jnp.asarray(ov_np, BF16)
    expand = jnp.asarray(expand_np, BF16)
    tri = jnp.asarray(tri_np, BF16)

    w_in_p = _permute_w_in(w_in)
    w_out_b = w_out.astype(BF16)
    wg_b = w_ffn_gate.astype(BF16)
    wu_b = w_ffn_up.astype(BF16)
    wd_b = w_ffn_down.astype(BF16)
    w1_b = cmp_w1.astype(BF16)
    w2_b = cmp_w2.astype(BF16)
    pos = cmp_pos.reshape(depth, 2, 1, CMP_LEN * HEAD_DIM)
    kgain2 = jnp.tile(k_gain, (1, 2))
    sgu_bx = jnp.repeat(jnp.swapaxes(sgu_b, 1, 2), HEAD_DIM, axis=2)

    x2 = x.reshape(m, D_MODEL)
    for l in range(depth):
        proj = _inproj(x2, norm_mix[l][None], w_in_p[l])
        hb = proj[:, COL_KV:COL_KV + 2 * LANES].astype(BF16)
        hb = hb.reshape(batch, nhb, CMP_STRIDE, 2, NSA_KV_HEADS, HEAD_DIM).transpose(0, 3, 4, 1, 2, 5)
        hb = hb.reshape(batch, 2, NSA_KV_HEADS, nhb, CMP_STRIDE * HEAD_DIM)
        kc, vc, ks, vs, kw, vw = _nsa_prep(proj, hb, w1_b[l], w2_b[l], pos[l], k_gain[l][None],
                                           kgain2[l][None], batch)
        o_a = _nsa_attention(proj, kc, vc, _key_tiles(ks, batch), _values_ones(vs, batch),
                             _key_tiles(kw, batch), _values_ones(vw, batch),
                             bias_c, bias_t, bias_w, overlap, expand, q_gain[l][None], batch)
        dk_t = proj[:, COL_DK:COL_DK + GROUP_WIDTH].astype(BF16)
        dk_t = dk_t.reshape(batch, N_KT, TK, GROUP_WIDTH).transpose(0, 1, 3, 2)
        dv = proj[:, COL_DV:COL_DV + GROUP_WIDTH].astype(BF16)
        o_d = _stickbreak(proj, dk_t, dv, tri, batch)
        o_b, o_c = _conv_sgu(proj, conv_w[l], sgu_w[l], sgu_bx[l])
        x2 = _outproj(o_a, o_b, o_c, o_d, group_gain[l].reshape(4, GROUP_WIDTH), w_out_b[l], x2)
        x2 = _ffn(x2, norm_ffn[l][None], wg_b[l], wu_b[l], wd_b[l])
    return x2.reshape(batch, seq, d_model)
```

```python
import math

import numpy as np
import jax
import jax.numpy as jnp
from jax import lax
from jax.experimental import pallas as pl
from jax.experimental.pallas import tpu as pltpu

F32 = jnp.float32
BF16 = jnp.bfloat16

D_MODEL = 2048
SEQ = 2048
HEAD_DIM = 64
GROUP_WIDTH = 512
N_HEADS = 8
NSA_KV_HEADS = 2
NSA_GQA = 4
CMP_LEN = 32
CMP_STRIDE = 16
SLC_LEN = 64
SLC_TOP = 16
WINDOW = 512
FORCE_BONUS = 1.0e3
N_BUCKETS = 32
MAX_DISTANCE = 1024
CONV_W = 3
CHUNK = 128
D_FF = 5632
NEG_INF = -1.0e30
N_CMP = (SEQ - CMP_LEN) // CMP_STRIDE + 1
N_SLC = SEQ // SLC_LEN
W_IN_COLS = 5400

LANES = 128
VMEM_LIMIT = 56 * 1024 * 1024

COL_Q = 0
COL_BG, COL_BC, COL_BH = 512, 1024, 1536
COL_CU, COL_CV = 2048, 2560
COL_DQ, COL_DK, COL_DV = 3072, 3584, 4096
COL_KV = 4608
COL_GATE = 5376
PROJ_COLS = 5632

TM_IN, TN_IN = 1024, 1408
TQ = 128
TK = 128
N_KT = SEQ // TK
WIN_TILES = WINDOW // TK + 1
TQ_SB = 256
TM_BC = 256
TM_OUT, TN_OUT = 1024, 1024
TM_FFN, TF_FFN = 1024, 512
MASKED_BUCKET = N_BUCKETS


def _params(*sem):
    return pltpu.CompilerParams(dimension_semantics=sem, vmem_limit_bytes=VMEM_LIMIT)


def _dot(a, b):
    return jnp.dot(a, b, preferred_element_type=F32)


def _dot_nt(a, b):
    return lax.dot_general(a, b, (((1,), (1,)), ((), ())), preferred_element_type=F32)


def _split_dot(a, b, terms):
    out = None
    rem = a
    for i in range(terms):
        part = rem.astype(BF16)
        d = _dot(part, b)
        out = d if out is None else out + d
        if i + 1 < terms:
            rem = rem - part.astype(F32)
    return out


def _gelu(x):
    c = math.sqrt(2.0 / math.pi)
    return x * (0.5 * (1.0 + jnp.tanh(c * (x + 0.044715 * (x * x * x)))))


def _rel_bucket_np(dist):
    n = np.maximum(dist, 0)
    max_exact = N_BUCKETS // 2
    nf = np.maximum(n, 1).astype(np.float64)
    large = max_exact + (np.log(nf / max_exact) / math.log(MAX_DISTANCE / max_exact)
                         * (N_BUCKETS - max_exact)).astype(np.int32)
    large = np.minimum(large, N_BUCKETS - 1)
    return np.where(n < max_exact, n, large).astype(np.int32)


def _bias_expand_kernel(tab_ref, bkt_ref, o_ref):
    bkt = bkt_ref[...]
    for h in range(N_HEADS):
        acc = jnp.full(bkt.shape, NEG_INF, F32)
        for b in range(N_BUCKETS):
            acc = jnp.where(bkt == b, tab_ref[b, h], acc)
        o_ref[h] = acc


def _bias_expand(rel_table, bucket_np):
    rows = bucket_np.shape[0]
    tr = 128
    return pl.pallas_call(
        _bias_expand_kernel,
        grid=(rows // tr,),
        in_specs=[pl.BlockSpec(memory_space=pltpu.SMEM),
                  pl.BlockSpec((tr, LANES), lambda i: (i, 0))],
        out_specs=pl.BlockSpec((N_HEADS, tr, LANES), lambda i: (0, i, 0)),
        out_shape=jax.ShapeDtypeStruct((N_HEADS, rows, LANES), F32),
        compiler_params=_params("parallel"),
        name="bias_expand",
    )(rel_table, jnp.asarray(bucket_np))


def _inproj_kernel(x_ref, g_ref, w_ref, o_ref, h_ref):
    @pl.when(pl.program_id(1) == 0)
    def _():
        x = x_ref[...]
        ms = jnp.mean(x * x, axis=-1, keepdims=True)
        h_ref[...] = ((x * lax.rsqrt(ms + 1e-6)) * g_ref[...]).astype(BF16)

    o_ref[...] = _dot(h_ref[...], w_ref[...])


def _inproj(x2, gain, w):
    m = x2.shape[0]
    return pl.pallas_call(
        _inproj_kernel,
        grid=(m // TM_IN, PROJ_COLS // TN_IN),
        in_specs=[pl.BlockSpec((TM_IN, D_MODEL), lambda i, j: (i, 0), pipeline_mode=pl.Buffered(1)),
                  pl.BlockSpec((1, D_MODEL), lambda i, j: (0, 0)),
                  pl.BlockSpec((D_MODEL, TN_IN), lambda i, j: (0, j))],
        out_specs=pl.BlockSpec((TM_IN, TN_IN), lambda i, j: (i, j)),
        out_shape=jax.ShapeDtypeStruct((m, PROJ_COLS), F32),
        scratch_shapes=[pltpu.VMEM((TM_IN, D_MODEL), BF16)],
        compiler_params=_params("parallel", "arbitrary"),
        name="inproj",
    )(x2, gain, w)


def _seg_rms(x, gain2):
    lane = lax.broadcasted_iota(jnp.int32, x.shape, 1)
    x2 = x * x
    lo = lane < HEAD_DIM
    s0 = jnp.sum(jnp.where(lo, x2, 0.0), axis=-1, keepdims=True)
    s1 = jnp.sum(jnp.where(lo, 0.0, x2), axis=-1, keepdims=True)
    ms = jnp.where(lo, s0, s1) * (1.0 / HEAD_DIM)
    return (x * lax.rsqrt(ms + 1e-6)) * gain2


def _nsa_prep_kernel(kv_ref, dk_ref, dv_ref, hb_ref, w1_ref, w2_ref, pos_ref, kg_ref, kg2_ref,
                     kc_ref, vc_ref, kst_ref, vsx_ref, kwt_ref, vwx_ref, dkt_ref, dvb_ref):
    kg2 = kg2_ref[...]
    lane = lax.broadcasted_iota(jnp.int32, (SEQ, LANES), 1)
    for col, kt_ref, vx_ref in ((2, kst_ref, vsx_ref), (4, kwt_ref, vwx_ref)):
        kn = _seg_rms(kv_ref[:, col * LANES:(col + 1) * LANES], kg2)
        for kj in range(N_KT):
            tile_t = kn[kj * TK:(kj + 1) * TK].T
            for g in range(NSA_KV_HEADS):
                kt_ref[0, g, kj] = tile_t[g * HEAD_DIM:(g + 1) * HEAD_DIM].astype(BF16)
        v = kv_ref[:, (col + 1) * LANES:(col + 2) * LANES]
        vx_ref[0, 0] = jnp.where(lane < HEAD_DIM, v, 1.0).astype(BF16)
        vx_ref[0, 1] = jnp.where(lane < HEAD_DIM, pltpu.roll(v, HEAD_DIM, axis=1), 1.0).astype(BF16)

    for kj in range(N_KT):
        dkt_ref[0, kj] = dk_ref[kj * TK:(kj + 1) * TK, :].T.astype(BF16)
    dvb_ref[...] = dv_ref[...].astype(BF16)

    half = CMP_STRIDE * HEAD_DIM
    for i, out_ref in ((0, kc_ref), (1, vc_ref)):
        w1 = w1_ref[i]
        pos = jnp.broadcast_to(pos_ref[i], (8, CMP_LEN * HEAD_DIM)).astype(BF16)
        posterm = _dot(pos, w1)[0:1]
        for g in range(NSA_KV_HEADS):
            hb = hb_ref[0, i, g]
            p1 = _dot(hb, w1[:half])
            p2 = _dot(hb, w1[half:])
            pre = p1 + pltpu.roll(p2, SEQ // CMP_STRIDE - 1, axis=0) + posterm
            out = _dot(_gelu(pre).astype(BF16), w2_ref[i])
            if i == 0:
                ms = jnp.mean(out * out, axis=-1, keepdims=True)
                out = (out * lax.rsqrt(ms + 1e-6)) * kg_ref[...]
            out_ref[0, g] = out.astype(BF16)


def _nsa_prep(proj, hb, w1, w2, pos, kgain, kgain2, batch):
    nhb = SEQ // CMP_STRIDE
    cmp_shape = jax.ShapeDtypeStruct((batch, NSA_KV_HEADS, nhb, HEAD_DIM), BF16)
    cmp_spec = pl.BlockSpec((1, NSA_KV_HEADS, nhb, HEAD_DIM), lambda b: (b, 0, 0, 0))
    kt_shape = jax.ShapeDtypeStruct((batch, NSA_KV_HEADS, N_KT, HEAD_DIM, TK), BF16)
    kt_spec = pl.BlockSpec((1, NSA_KV_HEADS, N_KT, HEAD_DIM, TK), lambda b: (b, 0, 0, 0, 0))
    vx_shape = jax.ShapeDtypeStruct((batch, NSA_KV_HEADS, SEQ, LANES), BF16)
    vx_spec = pl.BlockSpec((1, NSA_KV_HEADS, SEQ, LANES), lambda b: (b, 0, 0, 0))
    dkt_shape = jax.ShapeDtypeStruct((batch, N_KT, GROUP_WIDTH, TK), BF16)
    dkt_spec = pl.BlockSpec((1, N_KT, GROUP_WIDTH, TK), lambda b: (b, 0, 0, 0))
    dvb_shape = jax.ShapeDtypeStruct((batch * SEQ, GROUP_WIDTH), BF16)
    dvb_spec = pl.BlockSpec((SEQ, GROUP_WIDTH), lambda b: (b, 0))
    return pl.pallas_call(
        _nsa_prep_kernel,
        grid=(batch,),
        in_specs=[pl.BlockSpec((SEQ, 6 * LANES), lambda b: (b, COL_KV // (6 * LANES))),
                  pl.BlockSpec((SEQ, GROUP_WIDTH), lambda b: (b, COL_DK // GROUP_WIDTH)),
                  pl.BlockSpec((SEQ, GROUP_WIDTH), lambda b: (b, COL_DV // GROUP_WIDTH)),
                  pl.BlockSpec((1, 2, NSA_KV_HEADS, nhb, CMP_STRIDE * HEAD_DIM), lambda b: (b, 0, 0, 0, 0)),
                  pl.BlockSpec((2, CMP_LEN * HEAD_DIM, HEAD_DIM), lambda b: (0, 0, 0)),
                  pl.BlockSpec((2, HEAD_DIM, HEAD_DIM), lambda b: (0, 0, 0)),
                  pl.BlockSpec((2, 1, CMP_LEN * HEAD_DIM), lambda b: (0, 0, 0)),
                  pl.BlockSpec((1, HEAD_DIM), lambda b: (0, 0)),
                  pl.BlockSpec((1, LANES), lambda b: (0, 0))],
        out_specs=[cmp_spec, cmp_spec, kt_spec, vx_spec, kt_spec, vx_spec, dkt_spec, dvb_spec],
        out_shape=[cmp_shape, cmp_shape, kt_shape, vx_shape, kt_shape, vx_shape, dkt_shape, dvb_shape],
        compiler_params=_params("parallel"),
        name="nsa_prep",
    )(proj, proj, proj, hb, w1, w2, pos, kgain, kgain2)


def _nsa_kernel(q_ref, gate_ref, kc_ref, vc_ref, kst_ref, vsx_ref, kwt_ref, vwx_ref,
                bc_ref, bt_ref, bw_ref, ov_ref, ex_ref, gx_ref, qg_ref, o_ref, *scratch):
    n_chain = 2 * NSA_KV_HEADS
    m_refs, acc_refs = scratch[:n_chain], scratch[n_chain:]
    qi = pl.program_id(1)
    rows = NSA_GQA * TQ
    q = q_ref[...]
    qg = qg_ref[...] * (HEAD_DIM ** -0.5)
    qn = []
    for h in range(N_HEADS):
        seg = q[:, h * HEAD_DIM:(h + 1) * HEAD_DIM]
        ms = jnp.mean(seg * seg, axis=-1, keepdims=True)
        qn.append(((seg * lax.rsqrt(ms + 1e-6)) * qg).astype(BF16))
    qst = [jnp.concatenate(qn[g * NSA_GQA:(g + 1) * NSA_GQA], axis=0) for g in range(NSA_KV_HEADS)]
    gates = jax.nn.sigmoid(gate_ref[...])

    t1 = qi * TQ + lax.broadcasted_iota(jnp.int32, (TQ, LANES), 0)
    t4 = jnp.concatenate([t1] * NSA_GQA, axis=0)
    lane1 = lax.broadcasted_iota(jnp.int32, (TQ, LANES), 1)
    lane4 = lax.broadcasted_iota(jnp.int32, (rows, LANES), 1)
    blk_t = lax.broadcasted_iota(jnp.int32, (N_SLC, TQ), 0)

    o_cmp, sel = [], []
    for g in range(NSA_KV_HEADS):
        s = _dot_nt(qst[g], kc_ref[0, g]) + bc_ref[g * NSA_GQA:(g + 1) * NSA_GQA].reshape(rows, LANES)
        mask_c = (t4 - (lane4 * CMP_STRIDE + (CMP_LEN - 1)) >= 0) & (lane4 < N_CMP)
        s = jnp.where(mask_c, s, NEG_INF)
        m = jnp.max(s, axis=-1, keepdims=True)
        e = jnp.where(mask_c, jnp.exp(s - m), 0.0)
        l = jnp.sum(e, axis=-1, keepdims=True)
        p = e * jnp.where(l > 0.0, 1.0 / l, 0.0)
        o_cmp.append(_dot(p.astype(BF16), vc_ref[0, g]))

        psum = p[0:TQ] + p[TQ:2 * TQ] + p[2 * TQ:3 * TQ] + p[3 * TQ:4 * TQ]
        imp = _split_dot(psum, ov_ref[...], 3)
        cur = t1 >> 6
        valid = (lane1 * SLC_LEN <= t1) & (lane1 < N_SLC)
        forced = (lane1 == 0) | (lane1 == cur) | (lane1 == cur - 1)
        score = jnp.where(valid, imp + jnp.where(forced, FORCE_BONUS, 0.0), NEG_INF)
        st = score.T[0:N_SLC]
        rank = jnp.zeros((N_SLC, TQ), F32)
        for k in range(N_SLC):
            ck = st[k:k + 1, :]
            before = (ck > st) | ((ck == st) & (blk_t > k))
            rank = rank + jnp.where(before, 1.0, 0.0)
        sel_t = jnp.where(rank < float(SLC_TOP), 1.0, 0.0)
        sel_t = jnp.concatenate([sel_t, jnp.zeros((LANES - N_SLC, TQ), F32)], axis=0)
        sel.append(sel_t.T.astype(BF16))

    for c in range(n_chain):
        m_refs[c][...] = jnp.full((rows, LANES), NEG_INF, F32)
        acc_refs[c][...] = jnp.zeros((rows, LANES), F32)

    def flash_step(c, qk, bias, vx):
        sc = qk + bias
        m_old = m_refs[c][...]
        m_new = jnp.maximum(m_old, jnp.max(sc, axis=-1, keepdims=True))
        alpha = jnp.exp(m_old - m_new)
        pe = jnp.exp(sc - m_new)
        acc_refs[c][...] = alpha * acc_refs[c][...] + _dot(pe.astype(BF16), vx)
        m_refs[c][...] = m_new

    def slc_logits(kj, g):
        unsel = (_dot(sel[g], ex_ref[kj]) - 1.0) * (-NEG_INF)
        return _dot(qst[g], kst_ref[0, g, kj]), unsel

    def slc_update(kj, g, logits):
        qk, unsel = logits
        off = pl.multiple_of(kj * TK, TK)
        bias = bt_ref[qi - kj, g * NSA_GQA:(g + 1) * NSA_GQA]
        bias = (bias + unsel[None]).reshape(rows, TK)
        flash_step(g, qk, bias, vsx_ref[0, g, pl.ds(off, TK), :])

    def win_update(kj, g, qk):
        off = pl.multiple_of(kj * TK, TK)
        bias = bw_ref[qi - kj, g * NSA_GQA:(g + 1) * NSA_GQA].reshape(rows, TK)
        flash_step(NSA_KV_HEADS + g, qk, bias, vwx_ref[0, g, pl.ds(off, TK), :])

    lo = jnp.maximum(qi - (WIN_TILES - 1), 0)
    groups = range(NSA_KV_HEADS)

    def far_body(kj, carry):
        logits = [slc_logits(kj, g) for g in groups]
        for g in groups:
            slc_update(kj, g, logits[g])
        return carry

    lax.fori_loop(0, lo, far_body, 0)

    def near_body(kj, carry):
        s_logits = [slc_logits(kj, g) for g in groups]
        w_logits = [_dot(qst[g], kwt_ref[0, g, kj]) for g in groups]
        for g in groups:
            slc_update(kj, g, s_logits[g])
        for g in groups:
            win_update(kj, g, w_logits[g])
        return carry

    lax.fori_loop(lo, qi + 1, near_body, 0)

    def head_major(stacked):
        return jnp.concatenate([stacked[g][r * TQ:(r + 1) * TQ, :HEAD_DIM]
                                for g in groups for r in range(NSA_GQA)], axis=-1)

    def normalised(acc_ref):
        acc = acc_ref[...]
        return acc * (1.0 / pltpu.roll(acc, HEAD_DIM, axis=1))

    branches = (o_cmp,
                [normalised(acc_refs[g]) for g in groups],
                [normalised(acc_refs[NSA_KV_HEADS + g]) for g in groups])
    out = None
    for i, branch in enumerate(branches):
        term = _split_dot(gates, gx_ref[i], 3) * head_major(branch)
        out = term if out is None else out + term
    o_ref[...] = out


def _nsa_attention(proj, kc, vc, kst, vsx, kwt, vwx, bias_c, bias_t, bias_w, overlap, expand, gate_expand,
                   qgain, batch):
    nq = SEQ // TQ
    nhb = SEQ // CMP_STRIDE
    cmp_spec = pl.BlockSpec((1, NSA_KV_HEADS, nhb, HEAD_DIM), lambda b, i: (b, 0, 0, 0))
    kt_spec = pl.BlockSpec((1, NSA_KV_HEADS, N_KT, HEAD_DIM, TK), lambda b, i: (b, 0, 0, 0, 0))
    vx_spec = pl.BlockSpec((1, NSA_KV_HEADS, SEQ, LANES), lambda b, i: (b, 0, 0, 0))
    return pl.pallas_call(
        _nsa_kernel,
        grid=(batch, nq),
        in_specs=[pl.BlockSpec((TQ, GROUP_WIDTH), lambda b, i: (b * nq + i, COL_Q // GROUP_WIDTH)),
                  pl.BlockSpec((TQ, LANES), lambda b, i: (b * nq + i, COL_GATE // LANES)),
                  cmp_spec, cmp_spec, kt_spec, vx_spec, kt_spec, vx_spec,
                  pl.BlockSpec((N_HEADS, TQ, LANES), lambda b, i: (0, i, 0)),
                  pl.BlockSpec((N_KT, N_HEADS, TQ, TK), lambda b, i: (0, 0, 0, 0)),
                  pl.BlockSpec((WIN_TILES, N_HEADS, TQ, TK), lambda b, i: (0, 0, 0, 0)),
                  pl.BlockSpec((LANES, LANES), lambda b, i: (0, 0)),
                  pl.BlockSpec((N_KT, LANES, TK), lambda b, i: (0, 0, 0)),
                  pl.BlockSpec((3, LANES, GROUP_WIDTH), lambda b, i: (0, 0, 0)),
                  pl.BlockSpec((1, HEAD_DIM), lambda b, i: (0, 0))],
        out_specs=pl.BlockSpec((TQ, GROUP_WIDTH), lambda b, i: (b * nq + i, 0)),
        out_shape=jax.ShapeDtypeStruct((batch * SEQ, GROUP_WIDTH), F32),
        scratch_shapes=[pltpu.VMEM((NSA_GQA * TQ, LANES), F32)] * (4 * NSA_KV_HEADS),
        compiler_params=_params("parallel", "arbitrary"),
        name="nsa_attention",
    )(proj, proj, kc, vc, kst, vsx, kwt, vwx, bias_c, bias_t, bias_w, overlap, expand, gate_expand, qgain)


def _stickbreak_kernel(q_ref, kt_ref, v_ref, tri_ref, o_ref, *scratch):
    acc_ref, tc_ref = scratch[:N_HEADS], scratch[N_HEADS:]
    qi = pl.program_id(1)
    q = q_ref[...] * (HEAD_DIM ** -0.5)
    qh = [q[:, h * HEAD_DIM:(h + 1) * HEAD_DIM].astype(BF16) for h in range(N_HEADS)]
    t = qi * TQ_SB + lax.broadcasted_iota(jnp.int32, (TQ_SB, TK), 0)
    lane = lax.broadcasted_iota(jnp.int32, (TQ_SB, TK), 1)
    tri = tri_ref[...]
    for h in range(N_HEADS):
        acc_ref[h][...] = jnp.zeros((TQ_SB, LANES), F32)
        tc_ref[h][...] = jnp.zeros((TQ_SB, TK), F32)

    def tile_step(kj, masked):
        off = pl.multiple_of(kj * TK, TK)
        heads = range(N_HEADS)
        if masked:
            ok = (lane + off) < t
        z = [_dot(qh[h], kt_ref[0, kj, h * HEAD_DIM:(h + 1) * HEAD_DIM, :]) for h in heads]
        log_beta, r = [], []
        for h in heads:
            soft = jnp.log(1.0 + jnp.exp(-jnp.abs(z[h])))
            lb = jnp.minimum(z[h], 0.0) - soft
            log_1m = -jnp.maximum(z[h], 0.0) - soft
            if masked:
                log_1m = jnp.where(ok, log_1m, 0.0)
            log_beta.append(lb)
            r.append(_split_dot(log_1m, tri, 2))
        for h in heads:
            vt = v_ref[pl.ds(off, TK), (h // 2) * LANES:(h // 2 + 1) * LANES]
            a = jnp.exp(log_beta[h] + r[h][:, :TK] + tc_ref[h][...])
            if masked:
                a = jnp.where(ok, a, 0.0)
            acc_ref[h][...] += _dot(a.astype(BF16), vt)
            tc_ref[h][...] += r[h][:, TK:]

    n_diag = TQ_SB // TK
    for d in range(n_diag):
        tile_step((qi + 1) * n_diag - 1 - d, True)

    def body(i, carry):
        tile_step(qi * n_diag - 1 - i, False)
        return carry

    lax.fori_loop(0, qi * n_diag, body, 0)

    lane_o = lax.broadcasted_iota(jnp.int32, (TQ_SB, LANES), 1)
    for pair in range(N_HEADS // 2):
        o_ref[:, pair * LANES:(pair + 1) * LANES] = jnp.where(
            lane_o < HEAD_DIM, acc_ref[2 * pair][...], acc_ref[2 * pair + 1][...])


def _stickbreak(proj, kt, v, tri, batch):
    nq = SEQ // TQ_SB
    return pl.pallas_call(
        _stickbreak_kernel,
        grid=(batch, nq),
        in_specs=[pl.BlockSpec((TQ_SB, GROUP_WIDTH), lambda b, i: (b * nq + i, COL_DQ // GROUP_WIDTH)),
                  pl.BlockSpec((1, N_KT, GROUP_WIDTH, TK), lambda b, i: (b, 0, 0, 0)),
                  pl.BlockSpec((SEQ, GROUP_WIDTH), lambda b, i: (b, 0)),
                  pl.BlockSpec((TK, 2 * TK), lambda b, i: (0, 0))],
        out_specs=pl.BlockSpec((TQ_SB, GROUP_WIDTH), lambda b, i: (b * nq + i, 0)),
        out_shape=jax.ShapeDtypeStruct((batch * SEQ, GROUP_WIDTH), F32),
        scratch_shapes=[pltpu.VMEM((TQ_SB, LANES), F32)] * (2 * N_HEADS),
        compiler_params=_params("parallel", "arbitrary"),
        name="stickbreak",
    )(proj, kt, v, tri)


def _conv_sgu_kernel(bg_ref, cg_ref, hh_ref, cgp_ref, hhp_ref, cu_ref, cv_ref, cw_ref, sw_ref, sb_ref,
                     ob_ref, oc_ref):
    i = pl.program_id(0)
    z = cg_ref[...] * hh_ref[...]
    first = (i % (SEQ // TM_BC)) == 0
    zp = jnp.where(first, 0.0, cgp_ref[...] * hhp_ref[...])
    row = lax.broadcasted_iota(jnp.int32, z.shape, 0)
    p1 = jnp.broadcast_to(zp[7:8], z.shape)
    p2 = jnp.broadcast_to(zp[6:7], z.shape)
    z1 = jnp.where(row == 0, p1, pltpu.roll(z, 1, axis=0))
    z2 = jnp.where(row == 0, p2, jnp.where(row == 1, p1, pltpu.roll(z, 2, axis=0)))
    cw = cw_ref[...]
    y = cw[0:1] * z2
    y = y + cw[1:2] * z1
    y = y + cw[2:3] * z
    ob_ref[...] = bg_ref[...] * y

    u = _gelu(cu_ref[...])
    v = _gelu(cv_ref[...])
    mu = jnp.mean(v, axis=-1, keepdims=True)
    vc = v - mu
    var = jnp.mean(vc * vc, axis=-1, keepdims=True)
    vn = (vc * lax.rsqrt(var + 1e-5)).astype(BF16)
    pr = lax.broadcasted_iota(jnp.int32, (CHUNK, CHUNK), 0)
    pc = lax.broadcasted_iota(jnp.int32, (CHUNK, CHUNK), 1)
    ws = [jnp.where(pr >= pc, sw_ref[h], 0.0).astype(BF16) for h in range(N_HEADS)]
    for c in range(TM_BC // CHUNK):
        rs = slice(c * CHUNK, (c + 1) * CHUNK)
        s = jnp.concatenate(
            [_dot(ws[h], vn[rs, h * HEAD_DIM:(h + 1) * HEAD_DIM]) for h in range(N_HEADS)], axis=-1)
        oc_ref[rs, :] = u[rs] * (s + sb_ref[...])


def _conv_sgu(proj, conv_w, sgu_w, sgu_bx):
    m = proj.shape[0]
    gw = GROUP_WIDTH

    def col(c):
        return pl.BlockSpec((TM_BC, gw), lambda i: (i, c // gw))

    def prev(c):
        return pl.BlockSpec((8, gw), lambda i: (jnp.maximum(i * (TM_BC // 8) - 1, 0), c // gw))

    out = jax.ShapeDtypeStruct((m, gw), F32)
    return pl.pallas_call(
        _conv_sgu_kernel,
        grid=(m // TM_BC,),
        in_specs=[col(COL_BG), col(COL_BC), col(COL_BH), prev(COL_BC), prev(COL_BH), col(COL_CU), col(COL_CV),
                  pl.BlockSpec((CONV_W, gw), lambda i: (0, 0)),
                  pl.BlockSpec((N_HEADS, CHUNK, CHUNK), lambda i: (0, 0, 0)),
                  pl.BlockSpec((CHUNK, gw), lambda i: (0, 0))],
        out_specs=[pl.BlockSpec((TM_BC, gw), lambda i: (i, 0))] * 2,
        out_shape=[out, out],
        compiler_params=_params("parallel"),
        name="conv_sgu",
    )(proj, proj, proj, proj, proj, proj, proj, conv_w, sgu_w, sgu_bx)


def _outproj_kernel(oa_ref, ob_ref, oc_ref, od_ref, gg_ref, w_ref, x_ref, o_ref, mix_ref):
    @pl.when(pl.program_id(1) == 0)
    def _():
        for i, ref in enumerate((oa_ref, ob_ref, oc_ref, od_ref)):
            o = ref[...]
            ms = jnp.mean(o * o, axis=-1, keepdims=True)
            mix_ref[:, i * GROUP_WIDTH:(i + 1) * GROUP_WIDTH] = (
                (o * lax.rsqrt(ms + 1e-6)) * gg_ref[i:i + 1]).astype(BF16)

    o_ref[...] = x_ref[...] + _dot(mix_ref[...], w_ref[...])


def _outproj(oa, ob, oc, od, gg, w, x2):
    m = x2.shape[0]
    mix_spec = pl.BlockSpec((TM_OUT, GROUP_WIDTH), lambda i, j: (i, 0), pipeline_mode=pl.Buffered(1))
    return pl.pallas_call(
        _outproj_kernel,
        grid=(m // TM_OUT, D_MODEL // TN_OUT),
        in_specs=[mix_spec, mix_spec, mix_spec, mix_spec,
                  pl.BlockSpec((4, GROUP_WIDTH), lambda i, j: (0, 0)),
                  pl.BlockSpec((D_MODEL, TN_OUT), lambda i, j: (0, j)),
                  pl.BlockSpec((TM_OUT, TN_OUT), lambda i, j: (i, j))],
        out_specs=pl.BlockSpec((TM_OUT, TN_OUT), lambda i, j: (i, j)),
        out_shape=jax.ShapeDtypeStruct((m, D_MODEL), F32),
        scratch_shapes=[pltpu.VMEM((TM_OUT, D_MODEL), BF16)],
        compiler_params=_params("parallel", "arbitrary"),
        name="outproj",
    )(oa, ob, oc, od, gg, w, x2)


def _ffn_kernel(x_ref, g_ref, wg_ref, wu_ref, wd_ref, o_ref, h_ref):
    f = pl.program_id(1)

    @pl.when(f == 0)
    def _():
        x = x_ref[...]
        ms = jnp.mean(x * x, axis=-1, keepdims=True)
        h_ref[...] = ((x * lax.rsqrt(ms + 1e-6)) * g_ref[...]).astype(BF16)
        o_ref[...] = x

    h = h_ref[...]
    gate = _dot(h, wg_ref[...])
    up = _dot(h, wu_ref[...])
    act = (gate * jax.nn.sigmoid(gate)) * up
    o_ref[...] += _dot(act.astype(BF16), wd_ref[...])


def _ffn(x2, gain, wg, wu, wd):
    m = x2.shape[0]
    return pl.pallas_call(
        _ffn_kernel,
        grid=(m // TM_FFN, D_FF // TF_FFN),
        in_specs=[pl.BlockSpec((TM_FFN, D_MODEL), lambda i, f: (i, 0), pipeline_mode=pl.Buffered(1)),
                  pl.BlockSpec((1, D_MODEL), lambda i, f: (0, 0)),
                  pl.BlockSpec((D_MODEL, TF_FFN), lambda i, f: (0, f)),
                  pl.BlockSpec((D_MODEL, TF_FFN), lambda i, f: (0, f)),
                  pl.BlockSpec((TF_FFN, D_MODEL), lambda i, f: (f, 0))],
        out_specs=pl.BlockSpec((TM_FFN, D_MODEL), lambda i, f: (i, 0)),
        out_shape=jax.ShapeDtypeStruct((m, D_MODEL), F32),
        scratch_shapes=[pltpu.VMEM((TM_FFN, D_MODEL), BF16)],
        compiler_params=_params("parallel", "arbitrary"),
        name="ffn",
    )(x2, gain, wg, wu, wd)


def _static_tables():
    i = np.arange(TQ)[:, None]
    j = np.arange(TK)[None, :]
    dist_t = np.arange(N_KT)[:, None, None] * TK + i[None] - j[None]
    bkt_t = np.where(dist_t >= 0, _rel_bucket_np(dist_t), MASKED_BUCKET)
    dist_w = dist_t[:WIN_TILES]
    bkt_w = np.where((dist_w >= 0) & (dist_w < WINDOW), _rel_bucket_np(dist_w), MASKED_BUCKET)
    t = np.arange(SEQ)[:, None]
    n = np.arange(LANES)[None, :]
    bkt_c = _rel_bucket_np(t - (n * CMP_STRIDE + CMP_LEN - 1))
    c0 = np.arange(LANES)[:, None] * CMP_STRIDE
    s0 = np.arange(LANES)[None, :] * SLC_LEN
    ov = np.minimum(c0 + CMP_LEN, s0 + SLC_LEN) - np.maximum(c0, s0)
    ov = np.maximum(ov, 0) / CMP_LEN
    ov[N_CMP:, :] = 0.0
    ov[:, N_SLC:] = 0.0
    key_blk = (np.arange(N_KT)[:, None] * TK + np.arange(TK)[None, :]) // SLC_LEN
    expand = (np.arange(LANES)[None, :, None] == key_blk[:, None, :]).astype(np.float32)
    tri = (np.arange(TK)[:, None] > np.arange(TK)[None, :]).astype(np.float32)
    tri = np.concatenate([tri, np.ones((TK, TK), np.float32)], axis=1)
    gate_expand = np.zeros((3, LANES, GROUP_WIDTH), np.float32)
    for i in range(3):
        for h in range(N_HEADS):
            gate_expand[i, 3 * h + i, h * HEAD_DIM:(h + 1) * HEAD_DIM] = 1.0
    return (bkt_t.reshape(-1, TK).astype(np.int32), bkt_w.reshape(-1, TK).astype(np.int32), bkt_c,
            ov.astype(np.float32), expand, tri, gate_expand)


def _permute_w_in(w_in):
    depth = w_in.shape[0]
    pad = jnp.zeros((depth, D_MODEL, PROJ_COLS - W_IN_COLS), w_in.dtype)
    gate_pad, tail_pad = pad[..., :LANES - 3 * N_HEADS], pad[..., LANES - 3 * N_HEADS:]
    return jnp.concatenate([w_in[..., 0:512], w_in[..., 1304:5400], w_in[..., 512:1280],
                            w_in[..., 1280:1304], gate_pad, tail_pad], axis=-1).astype(BF16)


def kernel(x, w_in, w_out, norm_mix, norm_ffn, q_gain, k_gain, cmp_pos, cmp_w1, cmp_w2, rel_table, conv_w,
           sgu_w, sgu_b, group_gain, w_ffn_gate, w_ffn_up, w_ffn_down):
    batch, seq, d_model = x.shape
    assert seq == SEQ and d_model == D_MODEL
    depth = w_in.shape[0]
    m = batch * seq
    nhb = SEQ // CMP_STRIDE

    bkt_t, bkt_w, bkt_c, ov_np, expand_np, tri_np, gx_np = _static_tables()

    def tiles(b):
        return b.reshape(N_HEADS, -1, TQ, TK).transpose(1, 0, 2, 3)

    bias_t = tiles(_bias_expand(rel_table, bkt_t))
    bias_w = tiles(_bias_expand(rel_table, bkt_w))
    bias_c = _bias_expand(rel_table, bkt_c)
    overlap = jnp.asarray(ov_np, BF16)
    expand = jnp.asarray(expand_np, BF16)
    tri = jnp.asarray(tri_np, BF16)
    gate_expand = jnp.asarray(gx_np, BF16)

    w_in_p = _permute_w_in(w_in)
    w_out_b = w_out.astype(BF16)
    wg_b = w_ffn_gate.astype(BF16)
    wu_b = w_ffn_up.astype(BF16)
    wd_b = w_ffn_down.astype(BF16)
    w1_b = cmp_w1.astype(BF16)
    w2_b = cmp_w2.astype(BF16)
    pos = cmp_pos.reshape(depth, 2, 1, CMP_LEN * HEAD_DIM)
    kgain2 = jnp.tile(k_gain, (1, 2))
    sgu_bx = jnp.repeat(jnp.swapaxes(sgu_b, 1, 2), HEAD_DIM, axis=2)

    x2 = x.reshape(m, D_MODEL)
    for l in range(depth):
        proj = _inproj(x2, norm_mix[l][None], w_in_p[l])
        hb = proj[:, COL_KV:COL_KV + 2 * LANES].astype(BF16)
        hb = hb.reshape(batch, nhb, CMP_STRIDE, 2, NSA_KV_HEADS, HEAD_DIM).transpose(0, 3, 4, 1, 2, 5)
        hb = hb.reshape(batch, 2, NSA_KV_HEADS, nhb, CMP_STRIDE * HEAD_DIM)
        kc, vc, kst, vsx, kwt, vwx, dk_t, dv = _nsa_prep(proj, hb, w1_b[l], w2_b[l], pos[l], k_gain[l][None],
                                                         kgain2[l][None], batch)
        o_a = _nsa_attention(proj, kc, vc, kst, vsx, kwt, vwx,
                             bias_c, bias_t, bias_w, overlap, expand, gate_expand, q_gain[l][None], batch)
        o_d = _stickbreak(proj, dk_t, dv, tri, batch)
        o_b, o_c = _conv_sgu(proj, conv_w[l], sgu_w[l], sgu_bx[l])
        x2 = _outproj(o_a, o_b, o_c, o_d, group_gain[l].reshape(4, GROUP_WIDTH), w_out_b[l], x2)
        x2 = _ffn(x2, norm_ffn[l][None], wg_b[l], wu_b[l], wd_b[l])
    return x2.reshape(batch, seq, d_model)
```

```python
import math

import numpy as np
import jax
import jax.numpy as jnp
from jax import lax
from jax.experimental import pallas as pl
from jax.experimental.pallas import tpu as pltpu

F32 = jnp.float32
BF16 = jnp.bfloat16

D_MODEL = 2048
SEQ = 2048
HEAD_DIM = 64
GROUP_WIDTH = 512
N_HEADS = 8
NSA_KV_HEADS = 2
NSA_GQA = 4
CMP_LEN = 32
CMP_STRIDE = 16
SLC_LEN = 64
SLC_TOP = 16
WINDOW = 512
FORCE_BONUS = 1.0e3
N_BUCKETS = 32
MAX_DISTANCE = 1024
CONV_W = 3
CHUNK = 128
D_FF = 5632
NEG_INF = -1.0e30
N_CMP = (SEQ - CMP_LEN) // CMP_STRIDE + 1
N_SLC = SEQ // SLC_LEN
W_IN_COLS = 5400

LANES = 128
VMEM_LIMIT = 56 * 1024 * 1024

COL_Q = 0
COL_BG, COL_BC, COL_BH = 512, 1024, 1536
COL_CU, COL_CV = 2048, 2560
COL_DQ, COL_DK, COL_DV = 3072, 3584, 4096
COL_KV = 4608
COL_GATE = 5376
PROJ_COLS = 5632

TM_IN, TN_IN = 1024, 1408
TQ = 128
TK = 128
N_KT = SEQ // TK
WIN_TILES = WINDOW // TK + 1
TQ_SB = 256
TM_BC = 256
TM_OUT = 512
TM_FFN, TF_FFN = 1024, 512
MASKED_BUCKET = N_BUCKETS


def _params(*sem):
    return pltpu.CompilerParams(dimension_semantics=sem, vmem_limit_bytes=VMEM_LIMIT)


def _dot(a, b):
    return jnp.dot(a, b, preferred_element_type=F32)


def _dot_nt(a, b):
    return lax.dot_general(a, b, (((1,), (1,)), ((), ())), preferred_element_type=F32)


def _split_dot(a, b_stacked, terms):
    parts = []
    rem = a
    for i in range(terms):
        part = rem.astype(BF16)
        parts.append(part)
        if i + 1 < terms:
            rem = rem - part.astype(F32)
    return _dot(jnp.concatenate(parts, axis=1), b_stacked)


def _stack_rows(b_np, terms):
    return np.concatenate([b_np] * terms, axis=0)


def _gelu(x):
    c = math.sqrt(2.0 / math.pi)
    return x * (0.5 * (1.0 + jnp.tanh(c * (x + 0.044715 * (x * x * x)))))


def _rel_bucket_np(dist):
    n = np.maximum(dist, 0)
    max_exact = N_BUCKETS // 2
    nf = np.maximum(n, 1).astype(np.float64)
    large = max_exact + (np.log(nf / max_exact) / math.log(MAX_DISTANCE / max_exact)
                         * (N_BUCKETS - max_exact)).astype(np.int32)
    large = np.minimum(large, N_BUCKETS - 1)
    return np.where(n < max_exact, n, large).astype(np.int32)


def _bias_expand_kernel(tab_ref, bkt_ref, o_ref):
    bkt = bkt_ref[...]
    for h in range(N_HEADS):
        acc = jnp.full(bkt.shape, NEG_INF, F32)
        for b in range(N_BUCKETS):
            acc = jnp.where(bkt == b, tab_ref[b, h], acc)
        o_ref[h] = acc


def _bias_expand(rel_table, bucket_np):
    rows = bucket_np.shape[0]
    tr = 128
    return pl.pallas_call(
        _bias_expand_kernel,
        grid=(rows // tr,),
        in_specs=[pl.BlockSpec(memory_space=pltpu.SMEM),
                  pl.BlockSpec((tr, LANES), lambda i: (i, 0))],
        out_specs=pl.BlockSpec((N_HEADS, tr, LANES), lambda i: (0, i, 0)),
        out_shape=jax.ShapeDtypeStruct((N_HEADS, rows, LANES), F32),
        compiler_params=_params("parallel"),
        name="bias_expand",
    )(rel_table, jnp.asarray(bucket_np))


def _inproj_kernel(x_ref, g_ref, w_ref, o_ref, h_ref):
    @pl.when(pl.program_id(1) == 0)
    def _():
        x = x_ref[...]
        ms = jnp.mean(x * x, axis=-1, keepdims=True)
        h_ref[...] = ((x * lax.rsqrt(ms + 1e-6)) * g_ref[...]).astype(BF16)

    o_ref[...] = _dot(h_ref[...], w_ref[...])


def _inproj(x2, gain, w):
    m = x2.shape[0]
    return pl.pallas_call(
        _inproj_kernel,
        grid=(m // TM_IN, PROJ_COLS // TN_IN),
        in_specs=[pl.BlockSpec((TM_IN, D_MODEL), lambda i, j: (i, 0)),
                  pl.BlockSpec((1, D_MODEL), lambda i, j: (0, 0)),
                  pl.BlockSpec((D_MODEL, TN_IN), lambda i, j: (0, j))],
        out_specs=pl.BlockSpec((TM_IN, TN_IN), lambda i, j: (i, j)),
        out_shape=jax.ShapeDtypeStruct((m, PROJ_COLS), F32),
        scratch_shapes=[pltpu.VMEM((TM_IN, D_MODEL), BF16)],
        compiler_params=_params("parallel", "arbitrary"),
        name="inproj",
    )(x2, gain, w)


def _seg_rms(x, gain2):
    lane = lax.broadcasted_iota(jnp.int32, x.shape, 1)
    x2 = x * x
    lo = lane < HEAD_DIM
    s0 = jnp.sum(jnp.where(lo, x2, 0.0), axis=-1, keepdims=True)
    s1 = jnp.sum(jnp.where(lo, 0.0, x2), axis=-1, keepdims=True)
    ms = jnp.where(lo, s0, s1) * (1.0 / HEAD_DIM)
    return (x * lax.rsqrt(ms + 1e-6)) * gain2


def _nsa_prep_kernel(kv_ref, dk_ref, dv_ref, hb_ref, w1_ref, w2_ref, pos_ref, kg_ref, kg2_ref,
                     kc_ref, vc_ref, kst_ref, vsx_ref, kwt_ref, vwx_ref, dkt_ref, dvb_ref):
    kg2 = kg2_ref[...]
    lane = lax.broadcasted_iota(jnp.int32, (SEQ, LANES), 1)
    for col, kt_ref, vx_ref in ((2, kst_ref, vsx_ref), (4, kwt_ref, vwx_ref)):
        kn = _seg_rms(kv_ref[:, col * LANES:(col + 1) * LANES], kg2)
        for kj in range(N_KT):
            tile_t = kn[kj * TK:(kj + 1) * TK].T
            for g in range(NSA_KV_HEADS):
                kt_ref[0, g, kj] = tile_t[g * HEAD_DIM:(g + 1) * HEAD_DIM].astype(BF16)
        v = kv_ref[:, (col + 1) * LANES:(col + 2) * LANES]
        vx_ref[0, 0] = jnp.where(lane < HEAD_DIM, v, 1.0).astype(BF16)
        vx_ref[0, 1] = jnp.where(lane < HEAD_DIM, pltpu.roll(v, HEAD_DIM, axis=1), 1.0).astype(BF16)

    for kj in range(N_KT):
        dkt_ref[0, kj] = dk_ref[kj * TK:(kj + 1) * TK, :].T.astype(BF16)
    dvb_ref[...] = dv_ref[...].astype(BF16)

    half = CMP_STRIDE * HEAD_DIM
    for i, out_ref in ((0, kc_ref), (1, vc_ref)):
        w1 = w1_ref[i]
        pos = jnp.broadcast_to(pos_ref[i], (8, CMP_LEN * HEAD_DIM)).astype(BF16)
        posterm = _dot(pos, w1)[0:1]
        for g in range(NSA_KV_HEADS):
            hb = hb_ref[0, i, g]
            p1 = _dot(hb, w1[:half])
            p2 = _dot(hb, w1[half:])
            pre = p1 + pltpu.roll(p2, SEQ // CMP_STRIDE - 1, axis=0) + posterm
            out = _dot(_gelu(pre).astype(BF16), w2_ref[i])
            if i == 0:
                ms = jnp.mean(out * out, axis=-1, keepdims=True)
                out = (out * lax.rsqrt(ms + 1e-6)) * kg_ref[...]
            out_ref[0, g] = out.astype(BF16)


def _nsa_prep(proj, hb, w1, w2, pos, kgain, kgain2, batch):
    nhb = SEQ // CMP_STRIDE
    cmp_shape = jax.ShapeDtypeStruct((batch, NSA_KV_HEADS, nhb, HEAD_DIM), BF16)
    cmp_spec = pl.BlockSpec((1, NSA_KV_HEADS, nhb, HEAD_DIM), lambda b: (b, 0, 0, 0))
    kt_shape = jax.ShapeDtypeStruct((batch, NSA_KV_HEADS, N_KT, HEAD_DIM, TK), BF16)
    kt_spec = pl.BlockSpec((1, NSA_KV_HEADS, N_KT, HEAD_DIM, TK), lambda b: (b, 0, 0, 0, 0))
    vx_shape = jax.ShapeDtypeStruct((batch, NSA_KV_HEADS, SEQ, LANES), BF16)
    vx_spec = pl.BlockSpec((1, NSA_KV_HEADS, SEQ, LANES), lambda b: (b, 0, 0, 0))
    dkt_shape = jax.ShapeDtypeStruct((batch, N_KT, GROUP_WIDTH, TK), BF16)
    dkt_spec = pl.BlockSpec((1, N_KT, GROUP_WIDTH, TK), lambda b: (b, 0, 0, 0))
    dvb_shape = jax.ShapeDtypeStruct((batch * SEQ, GROUP_WIDTH), BF16)
    dvb_spec = pl.BlockSpec((SEQ, GROUP_WIDTH), lambda b: (b, 0))
    return pl.pallas_call(
        _nsa_prep_kernel,
        grid=(batch,),
        in_specs=[pl.BlockSpec((SEQ, 6 * LANES), lambda b: (b, COL_KV // (6 * LANES))),
                  pl.BlockSpec((SEQ, GROUP_WIDTH), lambda b: (b, COL_DK // GROUP_WIDTH)),
                  pl.BlockSpec((SEQ, GROUP_WIDTH), lambda b: (b, COL_DV // GROUP_WIDTH)),
                  pl.BlockSpec((1, 2, NSA_KV_HEADS, nhb, CMP_STRIDE * HEAD_DIM), lambda b: (b, 0, 0, 0, 0)),
                  pl.BlockSpec((2, CMP_LEN * HEAD_DIM, HEAD_DIM), lambda b: (0, 0, 0)),
                  pl.BlockSpec((2, HEAD_DIM, HEAD_DIM), lambda b: (0, 0, 0)),
                  pl.BlockSpec((2, 1, CMP_LEN * HEAD_DIM), lambda b: (0, 0, 0)),
                  pl.BlockSpec((1, HEAD_DIM), lambda b: (0, 0)),
                  pl.BlockSpec((1, LANES), lambda b: (0, 0))],
        out_specs=[cmp_spec, cmp_spec, kt_spec, vx_spec, kt_spec, vx_spec, dkt_spec, dvb_spec],
        out_shape=[cmp_shape, cmp_shape, kt_shape, vx_shape, kt_shape, vx_shape, dkt_shape, dvb_shape],
        compiler_params=_params("parallel"),
        name="nsa_prep",
    )(proj, proj, proj, hb, w1, w2, pos, kgain, kgain2)


def _nsa_kernel(q_ref, gate_ref, kc_ref, vc_ref, kst_ref, vsx_ref, kwt_ref, vwx_ref,
                bc_ref, bt_ref, bw_ref, ov_ref, ex_ref, gx_ref, qg_ref, o_ref, *scratch):
    n_chain = 2 * NSA_KV_HEADS
    m_refs, acc_refs = scratch[:n_chain], scratch[n_chain:]
    qi = pl.program_id(1)
    rows = NSA_GQA * TQ
    q = q_ref[...]
    qg = qg_ref[...] * (HEAD_DIM ** -0.5)
    qn = []
    for h in range(N_HEADS):
        seg = q[:, h * HEAD_DIM:(h + 1) * HEAD_DIM]
        ms = jnp.mean(seg * seg, axis=-1, keepdims=True)
        qn.append(((seg * lax.rsqrt(ms + 1e-6)) * qg).astype(BF16))
    qst = [jnp.concatenate(qn[g * NSA_GQA:(g + 1) * NSA_GQA], axis=0) for g in range(NSA_KV_HEADS)]
    gates = jax.nn.sigmoid(gate_ref[...])

    t1 = qi * TQ + lax.broadcasted_iota(jnp.int32, (TQ, LANES), 0)
    t4 = jnp.concatenate([t1] * NSA_GQA, axis=0)
    lane1 = lax.broadcasted_iota(jnp.int32, (TQ, LANES), 1)
    lane4 = lax.broadcasted_iota(jnp.int32, (rows, LANES), 1)
    blk_t = lax.broadcasted_iota(jnp.int32, (N_SLC, TQ), 0)

    o_cmp, sel = [], []
    for g in range(NSA_KV_HEADS):
        s = _dot_nt(qst[g], kc_ref[0, g]) + bc_ref[g * NSA_GQA:(g + 1) * NSA_GQA].reshape(rows, LANES)
        mask_c = (t4 - (lane4 * CMP_STRIDE + (CMP_LEN - 1)) >= 0) & (lane4 < N_CMP)
        s = jnp.where(mask_c, s, NEG_INF)
        m = jnp.max(s, axis=-1, keepdims=True)
        e = jnp.where(mask_c, jnp.exp(s - m), 0.0)
        l = jnp.sum(e, axis=-1, keepdims=True)
        p = e * jnp.where(l > 0.0, 1.0 / l, 0.0)
        o_cmp.append(_dot(p.astype(BF16), vc_ref[0, g]))

        psum = p[0:TQ] + p[TQ:2 * TQ] + p[2 * TQ:3 * TQ] + p[3 * TQ:4 * TQ]
        imp = _split_dot(psum, ov_ref[...], 3)
        cur = t1 >> 6
        valid = (lane1 * SLC_LEN <= t1) & (lane1 < N_SLC)
        forced = (lane1 == 0) | (lane1 == cur) | (lane1 == cur - 1)
        score = jnp.where(valid, imp + jnp.where(forced, FORCE_BONUS, 0.0), NEG_INF)
        st = score.T[0:N_SLC]
        rank = jnp.zeros((N_SLC, TQ), F32)
        for k in range(N_SLC):
            ck = st[k:k + 1, :]
            before = (ck > st) | ((ck == st) & (blk_t > k))
            rank = rank + jnp.where(before, 1.0, 0.0)
        sel_t = jnp.where(rank < float(SLC_TOP), 1.0, 0.0)
        sel_t = jnp.concatenate([sel_t, jnp.zeros((LANES - N_SLC, TQ), F32)], axis=0)
        sel.append(sel_t.T.astype(BF16))

    for c in range(n_chain):
        m_refs[c][...] = jnp.full((rows, LANES), NEG_INF, F32)
        acc_refs[c][...] = jnp.zeros((rows, LANES), F32)

    def flash_step(c, qk, bias, vx):
        sc = qk + bias
        m_old = m_refs[c][...]
        m_new = jnp.maximum(m_old, jnp.max(sc, axis=-1, keepdims=True))
        alpha = jnp.exp(m_old - m_new)
        pe = jnp.exp(sc - m_new)
        acc_refs[c][...] = alpha * acc_refs[c][...] + _dot(pe.astype(BF16), vx)
        m_refs[c][...] = m_new

    def slc_logits(kj, g):
        unsel = (_dot(sel[g], ex_ref[kj]) - 1.0) * (-NEG_INF)
        return _dot(qst[g], kst_ref[0, g, kj]), unsel

    def slc_update(kj, g, logits):
        qk, unsel = logits
        off = pl.multiple_of(kj * TK, TK)
        bias = bt_ref[qi - kj, g * NSA_GQA:(g + 1) * NSA_GQA]
        bias = (bias + unsel[None]).reshape(rows, TK)
        flash_step(g, qk, bias, vsx_ref[0, g, pl.ds(off, TK), :])

    def win_update(kj, g, qk):
        off = pl.multiple_of(kj * TK, TK)
        bias = bw_ref[qi - kj, g * NSA_GQA:(g + 1) * NSA_GQA].reshape(rows, TK)
        flash_step(NSA_KV_HEADS + g, qk, bias, vwx_ref[0, g, pl.ds(off, TK), :])

    lo = jnp.maximum(qi - (WIN_TILES - 1), 0)
    groups = range(NSA_KV_HEADS)

    def far_body(kj, carry):
        logits = [slc_logits(kj, g) for g in groups]
        for g in groups:
            slc_update(kj, g, logits[g])
        return carry

    lax.fori_loop(0, lo, far_body, 0)

    def near_body(kj, carry):
        s_logits = [slc_logits(kj, g) for g in groups]
        w_logits = [_dot(qst[g], kwt_ref[0, g, kj]) for g in groups]
        for g in groups:
            slc_update(kj, g, s_logits[g])
        for g in groups:
            win_update(kj, g, w_logits[g])
        return carry

    lax.fori_loop(lo, qi + 1, near_body, 0)

    def head_major(stacked):
        return jnp.concatenate([stacked[g][r * TQ:(r + 1) * TQ, :HEAD_DIM]
                                for g in groups for r in range(NSA_GQA)], axis=-1)

    def normalised(acc_ref):
        acc = acc_ref[...]
        return acc * (1.0 / pltpu.roll(acc, HEAD_DIM, axis=1))

    branches = (o_cmp,
                [normalised(acc_refs[g]) for g in groups],
                [normalised(acc_refs[NSA_KV_HEADS + g]) for g in groups])
    out = None
    for i, branch in enumerate(branches):
        term = _split_dot(gates, gx_ref[i], 3) * head_major(branch)
        out = term if out is None else out + term
    o_ref[...] = out


def _nsa_attention(proj, kc, vc, kst, vsx, kwt, vwx, bias_c, bias_t, bias_w, overlap, expand, gate_expand,
                   qgain, batch):
    nq = SEQ // TQ
    nhb = SEQ // CMP_STRIDE
    cmp_spec = pl.BlockSpec((1, NSA_KV_HEADS, nhb, HEAD_DIM), lambda b, i: (b, 0, 0, 0))
    kt_spec = pl.BlockSpec((1, NSA_KV_HEADS, N_KT, HEAD_DIM, TK), lambda b, i: (b, 0, 0, 0, 0))
    vx_spec = pl.BlockSpec((1, NSA_KV_HEADS, SEQ, LANES), lambda b, i: (b, 0, 0, 0))
    return pl.pallas_call(
        _nsa_kernel,
        grid=(batch, nq),
        in_specs=[pl.BlockSpec((TQ, GROUP_WIDTH), lambda b, i: (b * nq + i, COL_Q // GROUP_WIDTH)),
                  pl.BlockSpec((TQ, LANES), lambda b, i: (b * nq + i, COL_GATE // LANES)),
                  cmp_spec, cmp_spec, kt_spec, vx_spec, kt_spec, vx_spec,
                  pl.BlockSpec((N_HEADS, TQ, LANES), lambda b, i: (0, i, 0)),
                  pl.BlockSpec((N_KT, N_HEADS, TQ, TK), lambda b, i: (0, 0, 0, 0)),
                  pl.BlockSpec((WIN_TILES, N_HEADS, TQ, TK), lambda b, i: (0, 0, 0, 0)),
                  pl.BlockSpec((3 * LANES, LANES), lambda b, i: (0, 0)),
                  pl.BlockSpec((N_KT, LANES, TK), lambda b, i: (0, 0, 0)),
                  pl.BlockSpec((3, 3 * LANES, GROUP_WIDTH), lambda b, i: (0, 0, 0)),
                  pl.BlockSpec((1, HEAD_DIM), lambda b, i: (0, 0))],
        out_specs=pl.BlockSpec((TQ, GROUP_WIDTH), lambda b, i: (b * nq + i, 0)),
        out_shape=jax.ShapeDtypeStruct((batch * SEQ, GROUP_WIDTH), F32),
        scratch_shapes=[pltpu.VMEM((NSA_GQA * TQ, LANES), F32)] * (4 * NSA_KV_HEADS),
        compiler_params=_params("parallel", "arbitrary"),
        name="nsa_attention",
    )(proj, proj, kc, vc, kst, vsx, kwt, vwx, bias_c, bias_t, bias_w, overlap, expand, gate_expand, qgain)


def _stickbreak_kernel(q_ref, kt_ref, v_ref, tri_ref, o_ref, *scratch):
    acc_ref, tc_ref = scratch[:N_HEADS], scratch[N_HEADS:]
    qi = pl.program_id(1)
    q = q_ref[...] * (HEAD_DIM ** -0.5)
    qh = [q[:, h * HEAD_DIM:(h + 1) * HEAD_DIM].astype(BF16) for h in range(N_HEADS)]
    t = qi * TQ_SB + lax.broadcasted_iota(jnp.int32, (TQ_SB, TK), 0)
    lane = lax.broadcasted_iota(jnp.int32, (TQ_SB, TK), 1)
    tri = tri_ref[...]
    for h in range(N_HEADS):
        acc_ref[h][...] = jnp.zeros((TQ_SB, LANES), F32)
        tc_ref[h][...] = jnp.zeros((TQ_SB, TK), F32)

    def tile_step(kj, masked):
        off = pl.multiple_of(kj * TK, TK)
        heads = range(N_HEADS)
        if masked:
            ok = (lane + off) < t
        z = [_dot(qh[h], kt_ref[0, kj, h * HEAD_DIM:(h + 1) * HEAD_DIM, :]) for h in heads]
        log_beta, r = [], []
        for h in heads:
            soft = jnp.log(1.0 + jnp.exp(-jnp.abs(z[h])))
            lb = jnp.minimum(z[h], 0.0) - soft
            log_1m = -jnp.maximum(z[h], 0.0) - soft
            if masked:
                log_1m = jnp.where(ok, log_1m, 0.0)
            log_beta.append(lb)
            r.append(_split_dot(log_1m, tri, 2))
        for h in heads:
            vt = v_ref[pl.ds(off, TK), (h // 2) * LANES:(h // 2 + 1) * LANES]
            a = jnp.exp(log_beta[h] + r[h][:, :TK] + tc_ref[h][...])
            if masked:
                a = jnp.where(ok, a, 0.0)
            acc_ref[h][...] += _dot(a.astype(BF16), vt)
            tc_ref[h][...] += r[h][:, TK:]

    n_diag = TQ_SB // TK
    for d in range(n_diag):
        tile_step((qi + 1) * n_diag - 1 - d, True)

    def body(i, carry):
        tile_step(qi * n_diag - 1 - i, False)
        return carry

    lax.fori_loop(0, qi * n_diag, body, 0)

    lane_o = lax.broadcasted_iota(jnp.int32, (TQ_SB, LANES), 1)
    for pair in range(N_HEADS // 2):
        o_ref[:, pair * LANES:(pair + 1) * LANES] = jnp.where(
            lane_o < HEAD_DIM, acc_ref[2 * pair][...], acc_ref[2 * pair + 1][...])


def _stickbreak(proj, kt, v, tri, batch):
    nq = SEQ // TQ_SB
    return pl.pallas_call(
        _stickbreak_kernel,
        grid=(batch, nq),
        in_specs=[pl.BlockSpec((TQ_SB, GROUP_WIDTH), lambda b, i: (b * nq + i, COL_DQ // GROUP_WIDTH)),
                  pl.BlockSpec((1, N_KT, GROUP_WIDTH, TK), lambda b, i: (b, 0, 0, 0)),
                  pl.BlockSpec((SEQ, GROUP_WIDTH), lambda b, i: (b, 0)),
                  pl.BlockSpec((2 * TK, 2 * TK), lambda b, i: (0, 0))],
        out_specs=pl.BlockSpec((TQ_SB, GROUP_WIDTH), lambda b, i: (b * nq + i, 0)),
        out_shape=jax.ShapeDtypeStruct((batch * SEQ, GROUP_WIDTH), F32),
        scratch_shapes=[pltpu.VMEM((TQ_SB, LANES), F32)] * (2 * N_HEADS),
        compiler_params=_params("parallel", "arbitrary"),
        name="stickbreak",
    )(proj, kt, v, tri)


def _conv_sgu_kernel(bg_ref, cg_ref, hh_ref, cgp_ref, hhp_ref, cu_ref, cv_ref, cw_ref, sw_ref, sb_ref,
                     ob_ref, oc_ref):
    i = pl.program_id(0)
    z = cg_ref[...] * hh_ref[...]
    first = (i % (SEQ // TM_BC)) == 0
    zp = jnp.where(first, 0.0, cgp_ref[...] * hhp_ref[...])
    row = lax.broadcasted_iota(jnp.int32, z.shape, 0)
    p1 = jnp.broadcast_to(zp[7:8], z.shape)
    p2 = jnp.broadcast_to(zp[6:7], z.shape)
    z1 = jnp.where(row == 0, p1, pltpu.roll(z, 1, axis=0))
    z2 = jnp.where(row == 0, p2, jnp.where(row == 1, p1, pltpu.roll(z, 2, axis=0)))
    cw = cw_ref[...]
    y = cw[0:1] * z2
    y = y + cw[1:2] * z1
    y = y + cw[2:3] * z
    ob_ref[...] = bg_ref[...] * y

    u = _gelu(cu_ref[...])
    v = _gelu(cv_ref[...])
    mu = jnp.mean(v, axis=-1, keepdims=True)
    vc = v - mu
    var = jnp.mean(vc * vc, axis=-1, keepdims=True)
    vn = (vc * lax.rsqrt(var + 1e-5)).astype(BF16)
    pr = lax.broadcasted_iota(jnp.int32, (CHUNK, CHUNK), 0)
    pc = lax.broadcasted_iota(jnp.int32, (CHUNK, CHUNK), 1)
    ws = [jnp.where(pr >= pc, sw_ref[h], 0.0).astype(BF16) for h in range(N_HEADS)]
    for c in range(TM_BC // CHUNK):
        rs = slice(c * CHUNK, (c + 1) * CHUNK)
        s = jnp.concatenate(
            [_dot(ws[h], vn[rs, h * HEAD_DIM:(h + 1) * HEAD_DIM]) for h in range(N_HEADS)], axis=-1)
        oc_ref[rs, :] = u[rs] * (s + sb_ref[...])


def _conv_sgu(proj, conv_w, sgu_w, sgu_bx):
    m = proj.shape[0]
    gw = GROUP_WIDTH

    def col(c):
        return pl.BlockSpec((TM_BC, gw), lambda i: (i, c // gw))

    def prev(c):
        return pl.BlockSpec((8, gw), lambda i: (jnp.maximum(i * (TM_BC // 8) - 1, 0), c // gw))

    out = jax.ShapeDtypeStruct((m, gw), F32)
    return pl.pallas_call(
        _conv_sgu_kernel,
        grid=(m // TM_BC,),
        in_specs=[col(COL_BG), col(COL_BC), col(COL_BH), prev(COL_BC), prev(COL_BH), col(COL_CU), col(COL_CV),
                  pl.BlockSpec((CONV_W, gw), lambda i: (0, 0)),
                  pl.BlockSpec((N_HEADS, CHUNK, CHUNK), lambda i: (0, 0, 0)),
                  pl.BlockSpec((CHUNK, gw), lambda i: (0, 0))],
        out_specs=[pl.BlockSpec((TM_BC, gw), lambda i: (i, 0))] * 2,
        out_shape=[out, out],
        compiler_params=_params("parallel"),
        name="conv_sgu",
    )(proj, proj, proj, proj, proj, proj, proj, conv_w, sgu_w, sgu_bx)


def _outproj_kernel(oa_ref, ob_ref, oc_ref, od_ref, gg_ref, w_ref, x_ref, o_ref):
    parts = []
    for i, ref in enumerate((oa_ref, ob_ref, oc_ref, od_ref)):
        o = ref[...]
        ms = jnp.mean(o * o, axis=-1, keepdims=True)
        parts.append(((o * lax.rsqrt(ms + 1e-6)) * gg_ref[i:i + 1]).astype(BF16))
    o_ref[...] = x_ref[...] + _dot(jnp.concatenate(parts, axis=1), w_ref[...])


def _outproj(oa, ob, oc, od, gg, w, x2):
    m = x2.shape[0]
    mix_spec = pl.BlockSpec((TM_OUT, GROUP_WIDTH), lambda i: (i, 0))
    return pl.pallas_call(
        _outproj_kernel,
        grid=(m // TM_OUT,),
        in_specs=[mix_spec, mix_spec, mix_spec, mix_spec,
                  pl.BlockSpec((4, GROUP_WIDTH), lambda i: (0, 0)),
                  pl.BlockSpec((D_MODEL, D_MODEL), lambda i: (0, 0), pipeline_mode=pl.Buffered(1)),
                  pl.BlockSpec((TM_OUT, D_MODEL), lambda i: (i, 0))],
        out_specs=pl.BlockSpec((TM_OUT, D_MODEL), lambda i: (i, 0)),
        out_shape=jax.ShapeDtypeStruct((m, D_MODEL), F32),
        compiler_params=_params("parallel"),
        name="outproj",
    )(oa, ob, oc, od, gg, w, x2)


def _ffn_kernel(x_ref, g_ref, wg_ref, wu_ref, wd_ref, o_ref, h_ref):
    f = pl.program_id(1)

    @pl.when(f == 0)
    def _():
        x = x_ref[...]
        ms = jnp.mean(x * x, axis=-1, keepdims=True)
        h_ref[...] = ((x * lax.rsqrt(ms + 1e-6)) * g_ref[...]).astype(BF16)
        o_ref[...] = x

    h = h_ref[...]
    half = TF_FFN // 2
    pre = [(_dot(h, wg_ref[:, s:s + half]), _dot(h, wu_ref[:, s:s + half])) for s in (0, half)]
    down = None
    for s, (gate, up) in zip((0, half), pre):
        act = ((gate * jax.nn.sigmoid(gate)) * up).astype(BF16)
        d = _dot(act, wd_ref[s:s + half, :])
        down = d if down is None else down + d
    o_ref[...] += down


def _ffn(x2, gain, wg, wu, wd):
    m = x2.shape[0]
    return pl.pallas_call(
        _ffn_kernel,
        grid=(m // TM_FFN, D_FF // TF_FFN),
        in_specs=[pl.BlockSpec((TM_FFN, D_MODEL), lambda i, f: (i, 0)),
                  pl.BlockSpec((1, D_MODEL), lambda i, f: (0, 0)),
                  pl.BlockSpec((D_MODEL, TF_FFN), lambda i, f: (0, f)),
                  pl.BlockSpec((D_MODEL, TF_FFN), lambda i, f: (0, f)),
                  pl.BlockSpec((TF_FFN, D_MODEL), lambda i, f: (f, 0))],
        out_specs=pl.BlockSpec((TM_FFN, D_MODEL), lambda i, f: (i, 0)),
        out_shape=jax.ShapeDtypeStruct((m, D_MODEL), F32),
        scratch_shapes=[pltpu.VMEM((TM_FFN, D_MODEL), BF16)],
        compiler_params=_params("parallel", "arbitrary"),
        name="ffn",
    )(x2, gain, wg, wu, wd)


def _static_tables():
    i = np.arange(TQ)[:, None]
    j = np.arange(TK)[None, :]
    dist_t = np.arange(N_KT)[:, None, None] * TK + i[None] - j[None]
    bkt_t = np.where(dist_t >= 0, _rel_bucket_np(dist_t), MASKED_BUCKET)
    dist_w = dist_t[:WIN_TILES]
    bkt_w = np.where((dist_w >= 0) & (dist_w < WINDOW), _rel_bucket_np(dist_w), MASKED_BUCKET)
    t = np.arange(SEQ)[:, None]
    n = np.arange(LANES)[None, :]
    bkt_c = _rel_bucket_np(t - (n * CMP_STRIDE + CMP_LEN - 1))
    c0 = np.arange(LANES)[:, None] * CMP_STRIDE
    s0 = np.arange(LANES)[None, :] * SLC_LEN
    ov = np.minimum(c0 + CMP_LEN, s0 + SLC_LEN) - np.maximum(c0, s0)
    ov = np.maximum(ov, 0) / CMP_LEN
    ov[N_CMP:, :] = 0.0
    ov[:, N_SLC:] = 0.0
    key_blk = (np.arange(N_KT)[:, None] * TK + np.arange(TK)[None, :]) // SLC_LEN
    expand = (np.arange(LANES)[None, :, None] == key_blk[:, None, :]).astype(np.float32)
    tri = (np.arange(TK)[:, None] > np.arange(TK)[None, :]).astype(np.float32)
    tri = np.concatenate([tri, np.ones((TK, TK), np.float32)], axis=1)
    gate_expand = np.zeros((3, LANES, GROUP_WIDTH), np.float32)
    for i in range(3):
        for h in range(N_HEADS):
            gate_expand[i, 3 * h + i, h * HEAD_DIM:(h + 1) * HEAD_DIM] = 1.0
    return (bkt_t.reshape(-1, TK).astype(np.int32), bkt_w.reshape(-1, TK).astype(np.int32), bkt_c,
            ov.astype(np.float32), expand, tri, gate_expand)


def _permute_w_in(w_in):
    depth = w_in.shape[0]
    pad = jnp.zeros((depth, D_MODEL, PROJ_COLS - W_IN_COLS), w_in.dtype)
    gate_pad, tail_pad = pad[..., :LANES - 3 * N_HEADS], pad[..., LANES - 3 * N_HEADS:]
    return jnp.concatenate([w_in[..., 0:512], w_in[..., 1304:5400], w_in[..., 512:1280],
                            w_in[..., 1280:1304], gate_pad, tail_pad], axis=-1).astype(BF16)


def kernel(x, w_in, w_out, norm_mix, norm_ffn, q_gain, k_gain, cmp_pos, cmp_w1, cmp_w2, rel_table, conv_w,
           sgu_w, sgu_b, group_gain, w_ffn_gate, w_ffn_up, w_ffn_down):
    batch, seq, d_model = x.shape
    assert seq == SEQ and d_model == D_MODEL
    depth = w_in.shape[0]
    m = batch * seq
    nhb = SEQ // CMP_STRIDE

    bkt_t, bkt_w, bkt_c, ov_np, expand_np, tri_np, gx_np = _static_tables()

    def tiles(b):
        return b.reshape(N_HEADS, -1, TQ, TK).transpose(1, 0, 2, 3)

    bias_t = tiles(_bias_expand(rel_table, bkt_t))
    bias_w = tiles(_bias_expand(rel_table, bkt_w))
    bias_c = _bias_expand(rel_table, bkt_c)
    overlap = jnp.asarray(_stack_rows(ov_np, 3), BF16)
    expand = jnp.asarray(expand_np, BF16)
    tri = jnp.asarray(_stack_rows(tri_np, 2), BF16)
    gate_expand = jnp.asarray(np.concatenate([gx_np] * 3, axis=1), BF16)

    w_in_p = _permute_w_in(w_in)
    w_out_b = w_out.astype(BF16)
    wg_b = w_ffn_gate.astype(BF16)
    wu_b = w_ffn_up.astype(BF16)
    wd_b = w_ffn_down.astype(BF16)
    w1_b = cmp_w1.astype(BF16)
    w2_b = cmp_w2.astype(BF16)
    pos = cmp_pos.reshape(depth, 2, 1, CMP_LEN * HEAD_DIM)
    kgain2 = jnp.tile(k_gain, (1, 2))
    sgu_bx = jnp.repeat(jnp.swapaxes(sgu_b, 1, 2), HEAD_DIM, axis=2)

    x2 = x.reshape(m, D_MODEL)
    for l in range(depth):
        proj = _inproj(x2, norm_mix[l][None], w_in_p[l])
        hb = proj[:, COL_KV:COL_KV + 2 * LANES].astype(BF16)
        hb = hb.reshape(batch, nhb, CMP_STRIDE, 2, NSA_KV_HEADS, HEAD_DIM).transpose(0, 3, 4, 1, 2, 5)
        hb = hb.reshape(batch, 2, NSA_KV_HEADS, nhb, CMP_STRIDE * HEAD_DIM)
        kc, vc, kst, vsx, kwt, vwx, dk_t, dv = _nsa_prep(proj, hb, w1_b[l], w2_b[l], pos[l], k_gain[l][None],
                                                         kgain2[l][None], batch)
        o_a = _nsa_attention(proj, kc, vc, kst, vsx, kwt, vwx,
                             bias_c, bias_t, bias_w, overlap, expand, gate_expand, q_gain[l][None], batch)
        o_d = _stickbreak(proj, dk_t, dv, tri, batch)
        o_b, o_c = _conv_sgu(proj, conv_w[l], sgu_w[l], sgu_bx[l])
        x2 = _outproj(o_a, o_b, o_c, o_d, group_gain[l].reshape(4, GROUP_WIDTH), w_out_b[l], x2)
        x2 = _ffn(x2, norm_ffn[l][None], wg_b[l], wu_b[l], wd_b[l])
    return x2.reshape(batch, seq, d_model)
```

```python
import math

import numpy as np
import jax
import jax.numpy as jnp
from jax import lax
from jax.experimental import pallas as pl
from jax.experimental.pallas import tpu as pltpu

F32 = jnp.float32
BF16 = jnp.bfloat16

D_MODEL = 2048
SEQ = 2048
HEAD_DIM = 64
GROUP_WIDTH = 512
N_HEADS = 8
NSA_KV_HEADS = 2
NSA_GQA = 4
CMP_LEN = 32
CMP_STRIDE = 16
SLC_LEN = 64
SLC_TOP = 16
WINDOW = 512
FORCE_BONUS = 1.0e3
N_BUCKETS = 32
MAX_DISTANCE = 1024
CONV_W = 3
CHUNK = 128
D_FF = 5632
NEG_INF = -1.0e30
N_CMP = (SEQ - CMP_LEN) // CMP_STRIDE + 1
N_SLC = SEQ // SLC_LEN
W_IN_COLS = 5400

LANES = 128
VMEM_LIMIT = 56 * 1024 * 1024

COL_Q = 0
COL_BG, COL_BC, COL_BH = 512, 1024, 1536
COL_CU, COL_CV = 2048, 2560
COL_DQ, COL_DK, COL_DV = 3072, 3584, 4096
COL_KV = 4608
COL_GATE = 5376
PROJ_COLS = 5632

TM_IN, TN_IN = 1024, 1408
TQ = 128
TK = 128
N_KT = SEQ // TK
WIN_TILES = WINDOW // TK + 1
TQ_SB = 256
TM_BC = 256
TM_OUT = 512
TM_FFN, TF_FFN = 1024, 512
MASKED_BUCKET = N_BUCKETS
LOG2E = math.log2(math.e)


def _params(*sem):
    return pltpu.CompilerParams(dimension_semantics=sem, vmem_limit_bytes=VMEM_LIMIT)


def _dot(a, b):
    return jnp.dot(a, b, preferred_element_type=F32)


def _dot_nt(a, b):
    return lax.dot_general(a, b, (((1,), (1,)), ((), ())), preferred_element_type=F32)


def _split_dot(a, b_stacked, terms):
    parts = []
    rem = a
    for i in range(terms):
        part = rem.astype(BF16)
        parts.append(part)
        if i + 1 < terms:
            rem = rem - part.astype(F32)
    return _dot(jnp.concatenate(parts, axis=1), b_stacked)


def _stack_rows(b_np, terms):
    return np.concatenate([b_np] * terms, axis=0)


def _gelu(x):
    c = math.sqrt(2.0 / math.pi)
    return x * (0.5 * (1.0 + jnp.tanh(c * (x + 0.044715 * (x * x * x)))))


def _rel_bucket_np(dist):
    n = np.maximum(dist, 0)
    max_exact = N_BUCKETS // 2
    nf = np.maximum(n, 1).astype(np.float64)
    large = max_exact + (np.log(nf / max_exact) / math.log(MAX_DISTANCE / max_exact)
                         * (N_BUCKETS - max_exact)).astype(np.int32)
    large = np.minimum(large, N_BUCKETS - 1)
    return np.where(n < max_exact, n, large).astype(np.int32)


def _bias_expand_kernel(tab_ref, bkt_ref, o_ref):
    bkt = bkt_ref[...]
    for h in range(N_HEADS):
        acc = jnp.full(bkt.shape, NEG_INF, F32)
        for b in range(N_BUCKETS):
            acc = jnp.where(bkt == b, tab_ref[b, h] * LOG2E, acc)
        o_ref[h] = acc


def _bias_expand(rel_table, bucket_np):
    rows = bucket_np.shape[0]
    tr = 128
    return pl.pallas_call(
        _bias_expand_kernel,
        grid=(rows // tr,),
        in_specs=[pl.BlockSpec(memory_space=pltpu.SMEM),
                  pl.BlockSpec((tr, LANES), lambda i: (i, 0))],
        out_specs=pl.BlockSpec((N_HEADS, tr, LANES), lambda i: (0, i, 0)),
        out_shape=jax.ShapeDtypeStruct((N_HEADS, rows, LANES), F32),
        compiler_params=_params("parallel"),
        name="bias_expand",
    )(rel_table, jnp.asarray(bucket_np))


def _inproj_kernel(x_ref, g_ref, w_ref, o_ref, h_ref):
    @pl.when(pl.program_id(1) == 0)
    def _():
        x = x_ref[...]
        ms = jnp.mean(x * x, axis=-1, keepdims=True)
        h_ref[...] = ((x * lax.rsqrt(ms + 1e-6)) * g_ref[...]).astype(BF16)

    o_ref[...] = _dot(h_ref[...], w_ref[...])


def _inproj(x2, gain, w):
    m = x2.shape[0]
    return pl.pallas_call(
        _inproj_kernel,
        grid=(m // TM_IN, PROJ_COLS // TN_IN),
        in_specs=[pl.BlockSpec((TM_IN, D_MODEL), lambda i, j: (i, 0)),
                  pl.BlockSpec((1, D_MODEL), lambda i, j: (0, 0)),
                  pl.BlockSpec((D_MODEL, TN_IN), lambda i, j: (0, j))],
        out_specs=pl.BlockSpec((TM_IN, TN_IN), lambda i, j: (i, j)),
        out_shape=jax.ShapeDtypeStruct((m, PROJ_COLS), F32),
        scratch_shapes=[pltpu.VMEM((TM_IN, D_MODEL), BF16)],
        compiler_params=_params("parallel", "arbitrary"),
        name="inproj",
    )(x2, gain, w)


def _seg_rms(x, gain2):
    lane = lax.broadcasted_iota(jnp.int32, x.shape, 1)
    x2 = x * x
    lo = lane < HEAD_DIM
    s0 = jnp.sum(jnp.where(lo, x2, 0.0), axis=-1, keepdims=True)
    s1 = jnp.sum(jnp.where(lo, 0.0, x2), axis=-1, keepdims=True)
    ms = jnp.where(lo, s0, s1) * (1.0 / HEAD_DIM)
    return (x * lax.rsqrt(ms + 1e-6)) * gain2


def _nsa_prep_kernel(kv_ref, dk_ref, dv_ref, hb_ref, w1_ref, w2_ref, pos_ref, kg_ref, kg2_ref,
                     kc_ref, vc_ref, kst_ref, vsx_ref, kwt_ref, vwx_ref, dkt_ref, dvb_ref):
    kg2 = kg2_ref[...]
    lane = lax.broadcasted_iota(jnp.int32, (SEQ, LANES), 1)
    for col, kt_ref, vx_ref in ((2, kst_ref, vsx_ref), (4, kwt_ref, vwx_ref)):
        kn = _seg_rms(kv_ref[:, col * LANES:(col + 1) * LANES], kg2)
        for kj in range(N_KT):
            tile_t = kn[kj * TK:(kj + 1) * TK].T
            for g in range(NSA_KV_HEADS):
                kt_ref[0, g, kj] = tile_t[g * HEAD_DIM:(g + 1) * HEAD_DIM].astype(BF16)
        v = kv_ref[:, (col + 1) * LANES:(col + 2) * LANES]
        vx_ref[0, 0] = jnp.where(lane < HEAD_DIM, v, 1.0).astype(BF16)
        vx_ref[0, 1] = jnp.where(lane < HEAD_DIM, pltpu.roll(v, HEAD_DIM, axis=1), 1.0).astype(BF16)

    for kj in range(N_KT):
        dkt_ref[0, kj] = dk_ref[kj * TK:(kj + 1) * TK, :].T.astype(BF16)
    dvb_ref[...] = dv_ref[...].astype(BF16)

    half = CMP_STRIDE * HEAD_DIM
    for i, out_ref in ((0, kc_ref), (1, vc_ref)):
        w1 = w1_ref[i]
        pos = jnp.broadcast_to(pos_ref[i], (8, CMP_LEN * HEAD_DIM)).astype(BF16)
        posterm = _dot(pos, w1)[0:1]
        for g in range(NSA_KV_HEADS):
            hb = hb_ref[0, i, g]
            p1 = _dot(hb, w1[:half])
            p2 = _dot(hb, w1[half:])
            pre = p1 + pltpu.roll(p2, SEQ // CMP_STRIDE - 1, axis=0) + posterm
            out = _dot(_gelu(pre).astype(BF16), w2_ref[i])
            if i == 0:
                ms = jnp.mean(out * out, axis=-1, keepdims=True)
                out = (out * lax.rsqrt(ms + 1e-6)) * kg_ref[...]
            out_ref[0, g] = out.astype(BF16)


def _nsa_prep(proj, hb, w1, w2, pos, kgain, kgain2, batch):
    nhb = SEQ // CMP_STRIDE
    cmp_shape = jax.ShapeDtypeStruct((batch, NSA_KV_HEADS, nhb, HEAD_DIM), BF16)
    cmp_spec = pl.BlockSpec((1, NSA_KV_HEADS, nhb, HEAD_DIM), lambda b: (b, 0, 0, 0))
    kt_shape = jax.ShapeDtypeStruct((batch, NSA_KV_HEADS, N_KT, HEAD_DIM, TK), BF16)
    kt_spec = pl.BlockSpec((1, NSA_KV_HEADS, N_KT, HEAD_DIM, TK), lambda b: (b, 0, 0, 0, 0))
    vx_shape = jax.ShapeDtypeStruct((batch, NSA_KV_HEADS, SEQ, LANES), BF16)
    vx_spec = pl.BlockSpec((1, NSA_KV_HEADS, SEQ, LANES), lambda b: (b, 0, 0, 0))
    dkt_shape = jax.ShapeDtypeStruct((batch, N_KT, GROUP_WIDTH, TK), BF16)
    dkt_spec = pl.BlockSpec((1, N_KT, GROUP_WIDTH, TK), lambda b: (b, 0, 0, 0))
    dvb_shape = jax.ShapeDtypeStruct((batch * SEQ, GROUP_WIDTH), BF16)
    dvb_spec = pl.BlockSpec((SEQ, GROUP_WIDTH), lambda b: (b, 0))
    return pl.pallas_call(
        _nsa_prep_kernel,
        grid=(batch,),
        in_specs=[pl.BlockSpec((SEQ, 6 * LANES), lambda b: (b, COL_KV // (6 * LANES))),
                  pl.BlockSpec((SEQ, GROUP_WIDTH), lambda b: (b, COL_DK // GROUP_WIDTH)),
                  pl.BlockSpec((SEQ, GROUP_WIDTH), lambda b: (b, COL_DV // GROUP_WIDTH)),
                  pl.BlockSpec((1, 2, NSA_KV_HEADS, nhb, CMP_STRIDE * HEAD_DIM), lambda b: (b, 0, 0, 0, 0)),
                  pl.BlockSpec((2, CMP_LEN * HEAD_DIM, HEAD_DIM), lambda b: (0, 0, 0)),
                  pl.BlockSpec((2, HEAD_DIM, HEAD_DIM), lambda b: (0, 0, 0)),
                  pl.BlockSpec((2, 1, CMP_LEN * HEAD_DIM), lambda b: (0, 0, 0)),
                  pl.BlockSpec((1, HEAD_DIM), lambda b: (0, 0)),
                  pl.BlockSpec((1, LANES), lambda b: (0, 0))],
        out_specs=[cmp_spec, cmp_spec, kt_spec, vx_spec, kt_spec, vx_spec, dkt_spec, dvb_spec],
        out_shape=[cmp_shape, cmp_shape, kt_shape, vx_shape, kt_shape, vx_shape, dkt_shape, dvb_shape],
        compiler_params=_params("parallel"),
        name="nsa_prep",
    )(proj, proj, proj, hb, w1, w2, pos, kgain, kgain2)


def _nsa_kernel(q_ref, gate_ref, kc_ref, vc_ref, kst_ref, vsx_ref, kwt_ref, vwx_ref,
                bc_ref, bt_ref, bw_ref, ov_ref, ex_ref, gx_ref, qg_ref, o_ref, *scratch):
    n_chain = 2 * NSA_KV_HEADS
    m_refs, acc_refs = scratch[:n_chain], scratch[n_chain:]
    qi = pl.program_id(1)
    rows = NSA_GQA * TQ
    q = q_ref[...]
    qg = qg_ref[...] * (HEAD_DIM ** -0.5 * LOG2E)
    qn = []
    for h in range(N_HEADS):
        seg = q[:, h * HEAD_DIM:(h + 1) * HEAD_DIM]
        ms = jnp.mean(seg * seg, axis=-1, keepdims=True)
        qn.append(((seg * lax.rsqrt(ms + 1e-6)) * qg).astype(BF16))
    qst = [jnp.concatenate(qn[g * NSA_GQA:(g + 1) * NSA_GQA], axis=0) for g in range(NSA_KV_HEADS)]
    gates = jax.nn.sigmoid(gate_ref[...])

    t1 = qi * TQ + lax.broadcasted_iota(jnp.int32, (TQ, LANES), 0)
    t4 = jnp.concatenate([t1] * NSA_GQA, axis=0)
    lane1 = lax.broadcasted_iota(jnp.int32, (TQ, LANES), 1)
    lane4 = lax.broadcasted_iota(jnp.int32, (rows, LANES), 1)
    blk_t = lax.broadcasted_iota(jnp.int32, (N_SLC, TQ), 0)

    o_cmp, sel = [], []
    for g in range(NSA_KV_HEADS):
        s = _dot_nt(qst[g], kc_ref[0, g]) + bc_ref[g * NSA_GQA:(g + 1) * NSA_GQA].reshape(rows, LANES)
        mask_c = (t4 - (lane4 * CMP_STRIDE + (CMP_LEN - 1)) >= 0) & (lane4 < N_CMP)
        s = jnp.where(mask_c, s, NEG_INF)
        m = jnp.max(s, axis=-1, keepdims=True)
        e = jnp.where(mask_c, jnp.exp2(s - m), 0.0)
        l = jnp.sum(e, axis=-1, keepdims=True)
        p = e * jnp.where(l > 0.0, 1.0 / l, 0.0)
        o_cmp.append(_dot(p.astype(BF16), vc_ref[0, g]))

        psum = p[0:TQ] + p[TQ:2 * TQ] + p[2 * TQ:3 * TQ] + p[3 * TQ:4 * TQ]
        imp = _split_dot(psum, ov_ref[...], 3)
        cur = t1 >> 6
        valid = (lane1 * SLC_LEN <= t1) & (lane1 < N_SLC)
        forced = (lane1 == 0) | (lane1 == cur) | (lane1 == cur - 1)
        score = jnp.where(valid, imp + jnp.where(forced, FORCE_BONUS, 0.0), NEG_INF)
        st = score.T[0:N_SLC]
        rank = jnp.zeros((N_SLC, TQ), F32)
        for k in range(N_SLC):
            ck = st[k:k + 1, :]
            before = (ck > st) | ((ck == st) & (blk_t > k))
            rank = rank + jnp.where(before, 1.0, 0.0)
        sel_t = jnp.where(rank < float(SLC_TOP), 1.0, 0.0)
        sel_t = jnp.concatenate([sel_t, jnp.zeros((LANES - N_SLC, TQ), F32)], axis=0)
        sel.append(sel_t.T.astype(BF16))

    for c in range(n_chain):
        m_refs[c][...] = jnp.full((rows, LANES), NEG_INF, F32)
        acc_refs[c][...] = jnp.zeros((rows, LANES), F32)

    def flash_step(c, qk, bias, vx):
        sc = qk + bias
        m_old = m_refs[c][...]
        m_new = jnp.maximum(m_old, jnp.max(sc, axis=-1, keepdims=True))
        alpha = jnp.exp2(m_old - m_new)
        pe = jnp.exp2(sc - m_new)
        acc_refs[c][...] = alpha * acc_refs[c][...] + _dot(pe.astype(BF16), vx)
        m_refs[c][...] = m_new

    def slc_logits(kj, g):
        unsel = (_dot(sel[g], ex_ref[kj]) - 1.0) * (-NEG_INF)
        return _dot(qst[g], kst_ref[0, g, kj]), unsel

    def slc_update(kj, g, logits):
        qk, unsel = logits
        off = pl.multiple_of(kj * TK, TK)
        bias = bt_ref[qi - kj, g * NSA_GQA:(g + 1) * NSA_GQA]
        bias = (bias + unsel[None]).reshape(rows, TK)
        flash_step(g, qk, bias, vsx_ref[0, g, pl.ds(off, TK), :])

    def win_update(kj, g, qk):
        off = pl.multiple_of(kj * TK, TK)
        bias = bw_ref[qi - kj, g * NSA_GQA:(g + 1) * NSA_GQA].reshape(rows, TK)
        flash_step(NSA_KV_HEADS + g, qk, bias, vwx_ref[0, g, pl.ds(off, TK), :])

    lo = jnp.maximum(qi - (WIN_TILES - 1), 0)
    groups = range(NSA_KV_HEADS)

    def far_body(kj, carry):
        logits = [slc_logits(kj, g) for g in groups]
        for g in groups:
            slc_update(kj, g, logits[g])
        return carry

    lax.fori_loop(0, lo, far_body, 0)

    def near_body(kj, carry):
        s_logits = [slc_logits(kj, g) for g in groups]
        w_logits = [_dot(qst[g], kwt_ref[0, g, kj]) for g in groups]
        for g in groups:
            slc_update(kj, g, s_logits[g])
        for g in groups:
            win_update(kj, g, w_logits[g])
        return carry

    lax.fori_loop(lo, qi + 1, near_body, 0)

    def head_major(stacked):
        return jnp.concatenate([stacked[g][r * TQ:(r + 1) * TQ, :HEAD_DIM]
                                for g in groups for r in range(NSA_GQA)], axis=-1)

    def normalised(acc_ref):
        acc = acc_ref[...]
        return acc * (1.0 / pltpu.roll(acc, HEAD_DIM, axis=1))

    branches = (o_cmp,
                [normalised(acc_refs[g]) for g in groups],
                [normalised(acc_refs[NSA_KV_HEADS + g]) for g in groups])
    out = None
    for i, branch in enumerate(branches):
        term = _split_dot(gates, gx_ref[i], 3) * head_major(branch)
        out = term if out is None else out + term
    o_ref[...] = out


def _nsa_attention(proj, kc, vc, kst, vsx, kwt, vwx, bias_c, bias_t, bias_w, overlap, expand, gate_expand,
                   qgain, batch):
    nq = SEQ // TQ
    nhb = SEQ // CMP_STRIDE
    cmp_spec = pl.BlockSpec((1, NSA_KV_HEADS, nhb, HEAD_DIM), lambda b, i: (b, 0, 0, 0))
    kt_spec = pl.BlockSpec((1, NSA_KV_HEADS, N_KT, HEAD_DIM, TK), lambda b, i: (b, 0, 0, 0, 0))
    vx_spec = pl.BlockSpec((1, NSA_KV_HEADS, SEQ, LANES), lambda b, i: (b, 0, 0, 0))
    return pl.pallas_call(
        _nsa_kernel,
        grid=(batch, nq),
        in_specs=[pl.BlockSpec((TQ, GROUP_WIDTH), lambda b, i: (b * nq + i, COL_Q // GROUP_WIDTH)),
                  pl.BlockSpec((TQ, LANES), lambda b, i: (b * nq + i, COL_GATE // LANES)),
                  cmp_spec, cmp_spec, kt_spec, vx_spec, kt_spec, vx_spec,
                  pl.BlockSpec((N_HEADS, TQ, LANES), lambda b, i: (0, i, 0)),
                  pl.BlockSpec((N_KT, N_HEADS, TQ, TK), lambda b, i: (0, 0, 0, 0)),
                  pl.BlockSpec((WIN_TILES, N_HEADS, TQ, TK), lambda b, i: (0, 0, 0, 0)),
                  pl.BlockSpec((3 * LANES, LANES), lambda b, i: (0, 0)),
                  pl.BlockSpec((N_KT, LANES, TK), lambda b, i: (0, 0, 0)),
                  pl.BlockSpec((3, 3 * LANES, GROUP_WIDTH), lambda b, i: (0, 0, 0)),
                  pl.BlockSpec((1, HEAD_DIM), lambda b, i: (0, 0))],
        out_specs=pl.BlockSpec((TQ, GROUP_WIDTH), lambda b, i: (b * nq + i, 0)),
        out_shape=jax.ShapeDtypeStruct((batch * SEQ, GROUP_WIDTH), F32),
        scratch_shapes=[pltpu.VMEM((NSA_GQA * TQ, LANES), F32)] * (4 * NSA_KV_HEADS),
        compiler_params=_params("parallel", "arbitrary"),
        name="nsa_attention",
    )(proj, proj, kc, vc, kst, vsx, kwt, vwx, bias_c, bias_t, bias_w, overlap, expand, gate_expand, qgain)


def _stickbreak_kernel(q_ref, kt_ref, v_ref, tri_ref, o_ref, *scratch):
    acc_ref, tc_ref = scratch[:N_HEADS], scratch[N_HEADS:]
    qi = pl.program_id(1)
    q = q_ref[...] * (HEAD_DIM ** -0.5)
    qh = [q[:, h * HEAD_DIM:(h + 1) * HEAD_DIM].astype(BF16) for h in range(N_HEADS)]
    t = qi * TQ_SB + lax.broadcasted_iota(jnp.int32, (TQ_SB, TK), 0)
    lane = lax.broadcasted_iota(jnp.int32, (TQ_SB, TK), 1)
    tri = tri_ref[...]
    for h in range(N_HEADS):
        acc_ref[h][...] = jnp.zeros((TQ_SB, LANES), F32)
        tc_ref[h][...] = jnp.zeros((TQ_SB, TK), F32)

    def tile_step(kj, masked):
        off = pl.multiple_of(kj * TK, TK)
        heads = range(N_HEADS)
        if masked:
            ok = (lane + off) < t
        z = [_dot(qh[h], kt_ref[0, kj, h * HEAD_DIM:(h + 1) * HEAD_DIM, :]) for h in heads]
        log_beta, r = [], []
        for h in heads:
            soft = jnp.log(1.0 + jnp.exp2(jnp.abs(z[h]) * (-LOG2E)))
            lb = jnp.minimum(z[h], 0.0) - soft
            log_1m = lb - z[h]
            if masked:
                log_1m = jnp.where(ok, log_1m, 0.0)
            log_beta.append(lb)
            r.append(_split_dot(log_1m, tri, 2))
        for h in heads:
            vt = v_ref[pl.ds(off, TK), (h // 2) * LANES:(h // 2 + 1) * LANES]
            a = jnp.exp(log_beta[h] + r[h][:, :TK] + tc_ref[h][...])
            if masked:
                a = jnp.where(ok, a, 0.0)
            acc_ref[h][...] += _dot(a.astype(BF16), vt)
            tc_ref[h][...] += r[h][:, TK:]

    n_diag = TQ_SB // TK
    for d in range(n_diag):
        tile_step((qi + 1) * n_diag - 1 - d, True)

    def body(i, carry):
        tile_step(qi * n_diag - 1 - i, False)
        return carry

    lax.fori_loop(0, qi * n_diag, body, 0)

    lane_o = lax.broadcasted_iota(jnp.int32, (TQ_SB, LANES), 1)
    for pair in range(N_HEADS // 2):
        o_ref[:, pair * LANES:(pair + 1) * LANES] = jnp.where(
            lane_o < HEAD_DIM, acc_ref[2 * pair][...], acc_ref[2 * pair + 1][...])


def _stickbreak(proj, kt, v, tri, batch):
    nq = SEQ // TQ_SB
    return pl.pallas_call(
        _stickbreak_kernel,
        grid=(batch, nq),
        in_specs=[pl.BlockSpec((TQ_SB, GROUP_WIDTH), lambda b, i: (b * nq + i, COL_DQ // GROUP_WIDTH)),
                  pl.BlockSpec((1, N_KT, GROUP_WIDTH, TK), lambda b, i: (b, 0, 0, 0)),
                  pl.BlockSpec((SEQ, GROUP_WIDTH), lambda b, i: (b, 0)),
                  pl.BlockSpec((2 * TK, 2 * TK), lambda b, i: (0, 0))],
        out_specs=pl.BlockSpec((TQ_SB, GROUP_WIDTH), lambda b, i: (b * nq + i, 0)),
        out_shape=jax.ShapeDtypeStruct((batch * SEQ, GROUP_WIDTH), F32),
        scratch_shapes=[pltpu.VMEM((TQ_SB, LANES), F32)] * (2 * N_HEADS),
        compiler_params=_params("parallel", "arbitrary"),
        name="stickbreak",
    )(proj, kt, v, tri)


def _conv_sgu_kernel(bg_ref, cg_ref, hh_ref, cgp_ref, hhp_ref, cu_ref, cv_ref, cw_ref, sw_ref, sb_ref,
                     ob_ref, oc_ref):
    i = pl.program_id(0)
    z = cg_ref[...] * hh_ref[...]
    first = (i % (SEQ // TM_BC)) == 0
    zp = jnp.where(first, 0.0, cgp_ref[...] * hhp_ref[...])
    row = lax.broadcasted_iota(jnp.int32, z.shape, 0)
    p1 = jnp.broadcast_to(zp[7:8], z.shape)
    p2 = jnp.broadcast_to(zp[6:7], z.shape)
    z1 = jnp.where(row == 0, p1, pltpu.roll(z, 1, axis=0))
    z2 = jnp.where(row == 0, p2, jnp.where(row == 1, p1, pltpu.roll(z, 2, axis=0)))
    cw = cw_ref[...]
    y = cw[0:1] * z2
    y = y + cw[1:2] * z1
    y = y + cw[2:3] * z
    ob_ref[...] = bg_ref[...] * y

    u = _gelu(cu_ref[...])
    v = _gelu(cv_ref[...])
    mu = jnp.mean(v, axis=-1, keepdims=True)
    vc = v - mu
    var = jnp.mean(vc * vc, axis=-1, keepdims=True)
    vn = (vc * lax.rsqrt(var + 1e-5)).astype(BF16)
    pr = lax.broadcasted_iota(jnp.int32, (CHUNK, CHUNK), 0)
    pc = lax.broadcasted_iota(jnp.int32, (CHUNK, CHUNK), 1)
    ws = [jnp.where(pr >= pc, sw_ref[h], 0.0).astype(BF16) for h in range(N_HEADS)]
    for c in range(TM_BC // CHUNK):
        rs = slice(c * CHUNK, (c + 1) * CHUNK)
        s = jnp.concatenate(
            [_dot(ws[h], vn[rs, h * HEAD_DIM:(h + 1) * HEAD_DIM]) for h in range(N_HEADS)], axis=-1)
        oc_ref[rs, :] = u[rs] * (s + sb_ref[...])


def _conv_sgu(proj, conv_w, sgu_w, sgu_bx):
    m = proj.shape[0]
    gw = GROUP_WIDTH

    def col(c):
        return pl.BlockSpec((TM_BC, gw), lambda i: (i, c // gw))

    def prev(c):
        return pl.BlockSpec((8, gw), lambda i: (jnp.maximum(i * (TM_BC // 8) - 1, 0), c // gw))

    out = jax.ShapeDtypeStruct((m, gw), F32)
    return pl.pallas_call(
        _conv_sgu_kernel,
        grid=(m // TM_BC,),
        in_specs=[col(COL_BG), col(COL_BC), col(COL_BH), prev(COL_BC), prev(COL_BH), col(COL_CU), col(COL_CV),
                  pl.BlockSpec((CONV_W, gw), lambda i: (0, 0)),
                  pl.BlockSpec((N_HEADS, CHUNK, CHUNK), lambda i: (0, 0, 0)),
                  pl.BlockSpec((CHUNK, gw), lambda i: (0, 0))],
        out_specs=[pl.BlockSpec((TM_BC, gw), lambda i: (i, 0))] * 2,
        out_shape=[out, out],
        compiler_params=_params("parallel"),
        name="conv_sgu",
    )(proj, proj, proj, proj, proj, proj, proj, conv_w, sgu_w, sgu_bx)


def _outproj_kernel(oa_ref, ob_ref, oc_ref, od_ref, gg_ref, w_ref, x_ref, o_ref):
    parts = []
    for i, ref in enumerate((oa_ref, ob_ref, oc_ref, od_ref)):
        o = ref[...]
        ms = jnp.mean(o * o, axis=-1, keepdims=True)
        parts.append(((o * lax.rsqrt(ms + 1e-6)) * gg_ref[i:i + 1]).astype(BF16))
    o_ref[...] = x_ref[...] + _dot(jnp.concatenate(parts, axis=1), w_ref[...])


def _outproj(oa, ob, oc, od, gg, w, x2):
    m = x2.shape[0]
    mix_spec = pl.BlockSpec((TM_OUT, GROUP_WIDTH), lambda i: (i, 0))
    return pl.pallas_call(
        _outproj_kernel,
        grid=(m // TM_OUT,),
        in_specs=[mix_spec, mix_spec, mix_spec, mix_spec,
                  pl.BlockSpec((4, GROUP_WIDTH), lambda i: (0, 0)),
                  pl.BlockSpec((D_MODEL, D_MODEL), lambda i: (0, 0), pipeline_mode=pl.Buffered(1)),
                  pl.BlockSpec((TM_OUT, D_MODEL), lambda i: (i, 0))],
        out_specs=pl.BlockSpec((TM_OUT, D_MODEL), lambda i: (i, 0)),
        out_shape=jax.ShapeDtypeStruct((m, D_MODEL), F32),
        compiler_params=_params("parallel"),
        name="outproj",
    )(oa, ob, oc, od, gg, w, x2)


def _ffn_kernel(x_ref, g_ref, wg_ref, wu_ref, wd_ref, o_ref, h_ref):
    f = pl.program_id(1)

    @pl.when(f == 0)
    def _():
        x = x_ref[...]
        ms = jnp.mean(x * x, axis=-1, keepdims=True)
        h_ref[...] = ((x * lax.rsqrt(ms + 1e-6)) * g_ref[...]).astype(BF16)
        o_ref[...] = x

    h = h_ref[...]
    half = TF_FFN // 2
    pre = [(_dot(h, wg_ref[:, s:s + half]), _dot(h, wu_ref[:, s:s + half])) for s in (0, half)]
    down = None
    for s, (gate, up) in zip((0, half), pre):
        act = ((gate * jax.nn.sigmoid(gate)) * up).astype(BF16)
        d = _dot(act, wd_ref[s:s + half, :])
        down = d if down is None else down + d
    o_ref[...] += down


def _ffn(x2, gain, wg, wu, wd):
    m = x2.shape[0]
    return pl.pallas_call(
        _ffn_kernel,
        grid=(m // TM_FFN, D_FF // TF_FFN),
        in_specs=[pl.BlockSpec((TM_FFN, D_MODEL), lambda i, f: (i, 0)),
                  pl.BlockSpec((1, D_MODEL), lambda i, f: (0, 0)),
                  pl.BlockSpec((D_MODEL, TF_FFN), lambda i, f: (0, f)),
                  pl.BlockSpec((D_MODEL, TF_FFN), lambda i, f: (0, f)),
                  pl.BlockSpec((TF_FFN, D_MODEL), lambda i, f: (f, 0))],
        out_specs=pl.BlockSpec((TM_FFN, D_MODEL), lambda i, f: (i, 0)),
        out_shape=jax.ShapeDtypeStruct((m, D_MODEL), F32),
        scratch_shapes=[pltpu.VMEM((TM_FFN, D_MODEL), BF16)],
        compiler_params=_params("parallel", "arbitrary"),
        name="ffn",
    )(x2, gain, wg, wu, wd)


def _static_tables():
    i = np.arange(TQ)[:, None]
    j = np.arange(TK)[None, :]
    dist_t = np.arange(N_KT)[:, None, None] * TK + i[None] - j[None]
    bkt_t = np.where(dist_t >= 0, _rel_bucket_np(dist_t), MASKED_BUCKET)
    dist_w = dist_t[:WIN_TILES]
    bkt_w = np.where((dist_w >= 0) & (dist_w < WINDOW), _rel_bucket_np(dist_w), MASKED_BUCKET)
    t = np.arange(SEQ)[:, None]
    n = np.arange(LANES)[None, :]
    bkt_c = _rel_bucket_np(t - (n * CMP_STRIDE + CMP_LEN - 1))
    c0 = np.arange(LANES)[:, None] * CMP_STRIDE
    s0 = np.arange(LANES)[None, :] * SLC_LEN
    ov = np.minimum(c0 + CMP_LEN, s0 + SLC_LEN) - np.maximum(c0, s0)
    ov = np.maximum(ov, 0) / CMP_LEN
    ov[N_CMP:, :] = 0.0
    ov[:, N_SLC:] = 0.0
    key_blk = (np.arange(N_KT)[:, None] * TK + np.arange(TK)[None, :]) // SLC_LEN
    expand = (np.arange(LANES)[None, :, None] == key_blk[:, None, :]).astype(np.float32)
    tri = (np.arange(TK)[:, None] > np.arange(TK)[None, :]).astype(np.float32)
    tri = np.concatenate([tri, np.ones((TK, TK), np.float32)], axis=1)
    gate_expand = np.zeros((3, LANES, GROUP_WIDTH), np.float32)
    for i in range(3):
        for h in range(N_HEADS):
            gate_expand[i, 3 * h + i, h * HEAD_DIM:(h + 1) * HEAD_DIM] = 1.0
    return (bkt_t.reshape(-1, TK).astype(np.int32), bkt_w.reshape(-1, TK).astype(np.int32), bkt_c,
            ov.astype(np.float32), expand, tri, gate_expand)


def _permute_w_in(w_in):
    depth = w_in.shape[0]
    pad = jnp.zeros((depth, D_MODEL, PROJ_COLS - W_IN_COLS), w_in.dtype)
    gate_pad, tail_pad = pad[..., :LANES - 3 * N_HEADS], pad[..., LANES - 3 * N_HEADS:]
    return jnp.concatenate([w_in[..., 0:512], w_in[..., 1304:5400], w_in[..., 512:1280],
                            w_in[..., 1280:1304], gate_pad, tail_pad], axis=-1).astype(BF16)


def kernel(x, w_in, w_out, norm_mix, norm_ffn, q_gain, k_gain, cmp_pos, cmp_w1, cmp_w2, rel_table, conv_w,
           sgu_w, sgu_b, group_gain, w_ffn_gate, w_ffn_up, w_ffn_down):
    batch, seq, d_model = x.shape
    assert seq == SEQ and d_model == D_MODEL
    depth = w_in.shape[0]
    m = batch * seq
    nhb = SEQ // CMP_STRIDE

    bkt_t, bkt_w, bkt_c, ov_np, expand_np, tri_np, gx_np = _static_tables()

    def tiles(b):
        return b.reshape(N_HEADS, -1, TQ, TK).transpose(1, 0, 2, 3)

    bias_t = tiles(_bias_expand(rel_table, bkt_t))
    bias_w = tiles(_bias_expand(rel_table, bkt_w))
    bias_c = _bias_expand(rel_table, bkt_c)
    overlap = jnp.asarray(_stack_rows(ov_np, 3), BF16)
    expand = jnp.asarray(expand_np, BF16)
    tri = jnp.asarray(_stack_rows(tri_np, 2), BF16)
    gate_expand = jnp.asarray(np.concatenate([gx_np] * 3, axis=1), BF16)

    w_in_p = _permute_w_in(w_in)
    w_out_b = w_out.astype(BF16)
    wg_b = w_ffn_gate.astype(BF16)
    wu_b = w_ffn_up.astype(BF16)
    wd_b = w_ffn_down.astype(BF16)
    w1_b = cmp_w1.astype(BF16)
    w2_b = cmp_w2.astype(BF16)
    pos = cmp_pos.reshape(depth, 2, 1, CMP_LEN * HEAD_DIM)
    kgain2 = jnp.tile(k_gain, (1, 2))
    sgu_bx = jnp.repeat(jnp.swapaxes(sgu_b, 1, 2), HEAD_DIM, axis=2)

    x2 = x.reshape(m, D_MODEL)
    for l in range(depth):
        proj = _inproj(x2, norm_mix[l][None], w_in_p[l])
        hb = proj[:, COL_KV:COL_KV + 2 * LANES].astype(BF16)
        hb = hb.reshape(batch, nhb, CMP_STRIDE, 2, NSA_KV_HEADS, HEAD_DIM).transpose(0, 3, 4, 1, 2, 5)
        hb = hb.reshape(batch, 2, NSA_KV_HEADS, nhb, CMP_STRIDE * HEAD_DIM)
        kc, vc, kst, vsx, kwt, vwx, dk_t, dv = _nsa_prep(proj, hb, w1_b[l], w2_b[l], pos[l], k_gain[l][None],
                                                         kgain2[l][None], batch)
        o_a = _nsa_attention(proj, kc, vc, kst, vsx, kwt, vwx,
                             bias_c, bias_t, bias_w, overlap, expand, gate_expand, q_gain[l][None], batch)
        o_d = _stickbreak(proj, dk_t, dv, tri, batch)
        o_b, o_c = _conv_sgu(proj, conv_w[l], sgu_w[l], sgu_bx[l])
        x2 = _outproj(o_a, o_b, o_c, o_d, group_gain[l].reshape(4, GROUP_WIDTH), w_out_b[l], x2)
        x2 = _ffn(x2, norm_ffn[l][None], wg_b[l], wu_b[l], wd_b[l])
    return x2.reshape(batch, seq, d_model)
```

```python
import math

import numpy as np
import jax
import jax.numpy as jnp
from jax import lax
from jax.experimental import pallas as pl
from jax.experimental.pallas import tpu as pltpu

F32 = jnp.float32
BF16 = jnp.bfloat16

D_MODEL = 2048
SEQ = 2048
HEAD_DIM = 64
GROUP_WIDTH = 512
N_HEADS = 8
NSA_KV_HEADS = 2
NSA_GQA = 4
CMP_LEN = 32
CMP_STRIDE = 16
SLC_LEN = 64
SLC_TOP = 16
WINDOW = 512
FORCE_BONUS = 1.0e3
N_BUCKETS = 32
MAX_DISTANCE = 1024
CONV_W = 3
CHUNK = 128
D_FF = 5632
NEG_INF = -1.0e30
N_CMP = (SEQ - CMP_LEN) // CMP_STRIDE + 1
N_SLC = SEQ // SLC_LEN
W_IN_COLS = 5400

LANES = 128
VMEM_LIMIT = 56 * 1024 * 1024

COL_Q = 0
COL_BG, COL_BC, COL_BH = 512, 1024, 1536
COL_CU, COL_CV = 2048, 2560
COL_DQ, COL_DK, COL_DV = 3072, 3584, 4096
COL_KV = 4608
COL_GATE = 5376
PROJ_COLS = 5632

TM_IN, TN_IN = 1024, 1408
TQ = 128
TK = 128
N_KT = SEQ // TK
WIN_TILES = WINDOW // TK + 1
TQ_SB = 256
TM_BC = 256
TM_OUT = 512
TM_FFN, TF_FFN = 1024, 512
MASKED_BUCKET = N_BUCKETS
LOG2E = math.log2(math.e)


def _params(*sem):
    return pltpu.CompilerParams(dimension_semantics=sem, vmem_limit_bytes=VMEM_LIMIT)


def _dot(a, b):
    return jnp.dot(a, b, preferred_element_type=F32)


def _dot_nt(a, b):
    return lax.dot_general(a, b, (((1,), (1,)), ((), ())), preferred_element_type=F32)


def _split_dot(a, b_stacked, terms):
    parts = []
    rem = a
    for i in range(terms):
        part = rem.astype(BF16)
        parts.append(part)
        if i + 1 < terms:
            rem = rem - part.astype(F32)
    return _dot(jnp.concatenate(parts, axis=1), b_stacked)


def _stack_rows(b_np, terms):
    return np.concatenate([b_np] * terms, axis=0)


def _gelu(x):
    c = math.sqrt(2.0 / math.pi)
    return x * (0.5 * (1.0 + jnp.tanh(c * (x + 0.044715 * (x * x * x)))))


def _rel_bucket_np(dist):
    n = np.maximum(dist, 0)
    max_exact = N_BUCKETS // 2
    nf = np.maximum(n, 1).astype(np.float64)
    large = max_exact + (np.log(nf / max_exact) / math.log(MAX_DISTANCE / max_exact)
                         * (N_BUCKETS - max_exact)).astype(np.int32)
    large = np.minimum(large, N_BUCKETS - 1)
    return np.where(n < max_exact, n, large).astype(np.int32)


def _bias_expand_kernel(tab_ref, bkt_ref, o_ref):
    bkt = bkt_ref[...]
    for h in range(N_HEADS):
        acc = jnp.full(bkt.shape, NEG_INF, F32)
        for b in range(N_BUCKETS):
            acc = jnp.where(bkt == b, tab_ref[b, h] * LOG2E, acc)
        o_ref[h] = acc


def _bias_expand(rel_table, bucket_np):
    rows = bucket_np.shape[0]
    tr = 128
    return pl.pallas_call(
        _bias_expand_kernel,
        grid=(rows // tr,),
        in_specs=[pl.BlockSpec(memory_space=pltpu.SMEM),
                  pl.BlockSpec((tr, LANES), lambda i: (i, 0))],
        out_specs=pl.BlockSpec((N_HEADS, tr, LANES), lambda i: (0, i, 0)),
        out_shape=jax.ShapeDtypeStruct((N_HEADS, rows, LANES), F32),
        compiler_params=_params("parallel"),
        name="bias_expand",
    )(rel_table, jnp.asarray(bucket_np))


def _inproj_kernel(x_ref, g_ref, w_ref, o_ref, h_ref):
    @pl.when(pl.program_id(1) == 0)
    def _():
        x = x_ref[...]
        ms = jnp.mean(x * x, axis=-1, keepdims=True)
        h_ref[...] = ((x * lax.rsqrt(ms + 1e-6)) * g_ref[...]).astype(BF16)

    o_ref[...] = _dot(h_ref[...], w_ref[...])


def _inproj(x2, gain, w):
    m = x2.shape[0]
    return pl.pallas_call(
        _inproj_kernel,
        grid=(m // TM_IN, PROJ_COLS // TN_IN),
        in_specs=[pl.BlockSpec((TM_IN, D_MODEL), lambda i, j: (i, 0)),
                  pl.BlockSpec((1, D_MODEL), lambda i, j: (0, 0)),
                  pl.BlockSpec((D_MODEL, TN_IN), lambda i, j: (0, j))],
        out_specs=pl.BlockSpec((TM_IN, TN_IN), lambda i, j: (i, j)),
        out_shape=jax.ShapeDtypeStruct((m, PROJ_COLS), F32),
        scratch_shapes=[pltpu.VMEM((TM_IN, D_MODEL), BF16)],
        compiler_params=_params("parallel", "arbitrary"),
        name="inproj",
    )(x2, gain, w)


def _seg_rms(x, gain2):
    lane = lax.broadcasted_iota(jnp.int32, x.shape, 1)
    x2 = x * x
    lo = lane < HEAD_DIM
    s0 = jnp.sum(jnp.where(lo, x2, 0.0), axis=-1, keepdims=True)
    s1 = jnp.sum(jnp.where(lo, 0.0, x2), axis=-1, keepdims=True)
    ms = jnp.where(lo, s0, s1) * (1.0 / HEAD_DIM)
    return (x * lax.rsqrt(ms + 1e-6)) * gain2


def _nsa_prep_kernel(kv_ref, kcin_ref, vcin_ref, dk_ref, dv_ref, w1_ref, w2_ref, posterm_ref, kg2_ref,
                     kc_ref, vc_ref, kst_ref, vsx_ref, kwt_ref, vwx_ref, dkt_ref, dvb_ref):
    kg2 = kg2_ref[...]
    lane = lax.broadcasted_iota(jnp.int32, (SEQ, LANES), 1)
    for col, kt_ref, vx_ref in ((2, kst_ref, vsx_ref), (4, kwt_ref, vwx_ref)):
        kn = _seg_rms(kv_ref[:, col * LANES:(col + 1) * LANES], kg2)
        for kj in range(N_KT):
            tile_t = kn[kj * TK:(kj + 1) * TK].T
            for g in range(NSA_KV_HEADS):
                kt_ref[0, g, kj] = tile_t[g * HEAD_DIM:(g + 1) * HEAD_DIM].astype(BF16)
        v = kv_ref[:, (col + 1) * LANES:(col + 2) * LANES]
        vx_ref[0, 0] = jnp.where(lane < HEAD_DIM, v, 1.0).astype(BF16)
        vx_ref[0, 1] = jnp.where(lane < HEAD_DIM, pltpu.roll(v, HEAD_DIM, axis=1), 1.0).astype(BF16)

    for kj in range(N_KT):
        dkt_ref[0, kj] = dk_ref[kj * TK:(kj + 1) * TK, :].T.astype(BF16)
    dvb_ref[...] = dv_ref[...].astype(BF16)

    nhb = SEQ // CMP_STRIDE
    first = None
    second = None
    for l in range(CMP_STRIDE):
        rows_l = pl.ds(l, nhb, stride=CMP_STRIDE)
        x_l = jnp.concatenate([kcin_ref[rows_l, :], vcin_ref[rows_l, :]], axis=1).astype(BF16)
        a = _dot(x_l, w1_ref[0, l])
        b = _dot(x_l, w1_ref[1, l])
        first = a if first is None else first + a
        second = b if second is None else second + b
    pre = first + pltpu.roll(second, nhb - 1, axis=0) + posterm_ref[...]
    out = _dot(_gelu(pre).astype(BF16), w2_ref[...])
    keys = _seg_rms(out[:, :LANES], kg2).astype(BF16)
    vals = out[:, LANES:].astype(BF16)
    for g in range(NSA_KV_HEADS):
        kc_ref[0, g] = keys[:, g * HEAD_DIM:(g + 1) * HEAD_DIM]
        vc_ref[0, g] = vals[:, g * HEAD_DIM:(g + 1) * HEAD_DIM]


def _compress_weights(w1, w2):
    n_stream = 2 * NSA_KV_HEADS
    eye = jnp.eye(n_stream, dtype=w1.dtype)
    w1r = w1.reshape(2, 2, CMP_STRIDE, HEAD_DIM, HEAD_DIM).transpose(1, 2, 0, 3, 4)
    w1s = jnp.repeat(w1r, NSA_KV_HEADS, axis=2)
    w1_blk = jnp.einsum('hlcde,cf->hlcdfe', w1s, eye).reshape(2, CMP_STRIDE, n_stream * HEAD_DIM, n_stream * HEAD_DIM)
    w2s = jnp.repeat(w2, NSA_KV_HEADS, axis=0)
    w2_blk = jnp.einsum('cde,cf->cdfe', w2s, eye).reshape(n_stream * HEAD_DIM, n_stream * HEAD_DIM)
    return w1_blk.astype(BF16), w2_blk.astype(BF16)


def _posterm_kernel(pos_ref, w1_ref, o_ref):
    for i in range(2):
        pos = jnp.broadcast_to(pos_ref[i], (8, CMP_LEN * HEAD_DIM)).astype(BF16)
        term = _dot(pos, w1_ref[i])
        o_ref[:, i * LANES:(i + 1) * LANES] = jnp.concatenate([term] * NSA_KV_HEADS, axis=1)


def _posterm(pos, w1):
    return pl.pallas_call(
        _posterm_kernel,
        out_shape=jax.ShapeDtypeStruct((8, 2 * NSA_KV_HEADS * HEAD_DIM), F32),
        name="cmp_posterm",
    )(pos, w1)[0:1]


def _nsa_prep(proj, w1_blk, w2_blk, posterm, kgain2, batch):
    nhb = SEQ // CMP_STRIDE
    cmp_shape = jax.ShapeDtypeStruct((batch, NSA_KV_HEADS, nhb, HEAD_DIM), BF16)
    cmp_spec = pl.BlockSpec((1, NSA_KV_HEADS, nhb, HEAD_DIM), lambda b: (b, 0, 0, 0))
    kt_shape = jax.ShapeDtypeStruct((batch, NSA_KV_HEADS, N_KT, HEAD_DIM, TK), BF16)
    kt_spec = pl.BlockSpec((1, NSA_KV_HEADS, N_KT, HEAD_DIM, TK), lambda b: (b, 0, 0, 0, 0))
    vx_shape = jax.ShapeDtypeStruct((batch, NSA_KV_HEADS, SEQ, LANES), BF16)
    vx_spec = pl.BlockSpec((1, NSA_KV_HEADS, SEQ, LANES), lambda b: (b, 0, 0, 0))
    dkt_shape = jax.ShapeDtypeStruct((batch, N_KT, GROUP_WIDTH, TK), BF16)
    dkt_spec = pl.BlockSpec((1, N_KT, GROUP_WIDTH, TK), lambda b: (b, 0, 0, 0))
    dvb_shape = jax.ShapeDtypeStruct((batch * SEQ, GROUP_WIDTH), BF16)
    dvb_spec = pl.BlockSpec((SEQ, GROUP_WIDTH), lambda b: (b, 0))
    return pl.pallas_call(
        _nsa_prep_kernel,
        grid=(batch,),
        in_specs=[pl.BlockSpec((SEQ, 6 * LANES), lambda b: (b, COL_KV // (6 * LANES))),
                  pl.BlockSpec((SEQ, LANES), lambda b: (b, COL_KV // LANES)),
                  pl.BlockSpec((SEQ, LANES), lambda b: (b, COL_KV // LANES + 1)),
                  pl.BlockSpec((SEQ, GROUP_WIDTH), lambda b: (b, COL_DK // GROUP_WIDTH)),
                  pl.BlockSpec((SEQ, GROUP_WIDTH), lambda b: (b, COL_DV // GROUP_WIDTH)),
                  pl.BlockSpec((2, CMP_STRIDE, 2 * LANES, 2 * LANES), lambda b: (0, 0, 0, 0)),
                  pl.BlockSpec((2 * LANES, 2 * LANES), lambda b: (0, 0)),
                  pl.BlockSpec((1, 2 * LANES), lambda b: (0, 0)),
                  pl.BlockSpec((1, LANES), lambda b: (0, 0))],
        out_specs=[cmp_spec, cmp_spec, kt_spec, vx_spec, kt_spec, vx_spec, dkt_spec, dvb_spec],
        out_shape=[cmp_shape, cmp_shape, kt_shape, vx_shape, kt_shape, vx_shape, dkt_shape, dvb_shape],
        compiler_params=_params("parallel"),
        name="nsa_prep",
    )(proj, proj, proj, proj, proj, w1_blk, w2_blk, posterm, kgain2)


def _nsa_kernel(q_ref, gate_ref, kc_ref, vc_ref, kst_ref, vsx_ref, kwt_ref, vwx_ref,
                bc_ref, bt_ref, bw_ref, ov_ref, ex_ref, gx_ref, qg_ref, o_ref, *scratch):
    n_chain = 2 * NSA_KV_HEADS
    m_refs, acc_refs = scratch[:n_chain], scratch[n_chain:]
    qi = pl.program_id(1)
    rows = NSA_GQA * TQ
    q = q_ref[...]
    qg = qg_ref[...] * (HEAD_DIM ** -0.5 * LOG2E)
    qn = []
    for h in range(N_HEADS):
        seg = q[:, h * HEAD_DIM:(h + 1) * HEAD_DIM]
        ms = jnp.mean(seg * seg, axis=-1, keepdims=True)
        qn.append(((seg * lax.rsqrt(ms + 1e-6)) * qg).astype(BF16))
    qst = [jnp.concatenate(qn[g * NSA_GQA:(g + 1) * NSA_GQA], axis=0) for g in range(NSA_KV_HEADS)]
    gates = jax.nn.sigmoid(gate_ref[...])

    t1 = qi * TQ + lax.broadcasted_iota(jnp.int32, (TQ, LANES), 0)
    t4 = jnp.concatenate([t1] * NSA_GQA, axis=0)
    lane1 = lax.broadcasted_iota(jnp.int32, (TQ, LANES), 1)
    lane4 = lax.broadcasted_iota(jnp.int32, (rows, LANES), 1)
    blk_t = lax.broadcasted_iota(jnp.int32, (N_SLC, TQ), 0)

    o_cmp, sel = [], []
    for g in range(NSA_KV_HEADS):
        s = _dot_nt(qst[g], kc_ref[0, g]) + bc_ref[g * NSA_GQA:(g + 1) * NSA_GQA].reshape(rows, LANES)
        mask_c = (t4 - (lane4 * CMP_STRIDE + (CMP_LEN - 1)) >= 0) & (lane4 < N_CMP)
        s = jnp.where(mask_c, s, NEG_INF)
        m = jnp.max(s, axis=-1, keepdims=True)
        e = jnp.where(mask_c, jnp.exp2(s - m), 0.0)
        l = jnp.sum(e, axis=-1, keepdims=True)
        p = e * jnp.where(l > 0.0, 1.0 / l, 0.0)
        o_cmp.append(_dot(p.astype(BF16), vc_ref[0, g]))

        psum = p[0:TQ] + p[TQ:2 * TQ] + p[2 * TQ:3 * TQ] + p[3 * TQ:4 * TQ]
        imp = _split_dot(psum, ov_ref[...], 3)
        cur = t1 >> 6
        valid = (lane1 * SLC_LEN <= t1) & (lane1 < N_SLC)
        forced = (lane1 == 0) | (lane1 == cur) | (lane1 == cur - 1)
        score = jnp.where(valid, imp + jnp.where(forced, FORCE_BONUS, 0.0), NEG_INF)
        st = score.T[0:N_SLC]
        rank = jnp.zeros((N_SLC, TQ), F32)
        for k in range(N_SLC):
            ck = st[k:k + 1, :]
            before = (ck > st) | ((ck == st) & (blk_t > k))
            rank = rank + jnp.where(before, 1.0, 0.0)
        sel_t = jnp.where(rank < float(SLC_TOP), 1.0, 0.0)
        sel_t = jnp.concatenate([sel_t, jnp.zeros((LANES - N_SLC, TQ), F32)], axis=0)
        sel.append(sel_t.T.astype(BF16))

    for c in range(n_chain):
        m_refs[c][...] = jnp.full((rows, LANES), NEG_INF, F32)
        acc_refs[c][...] = jnp.zeros((rows, LANES), F32)

    def flash_step(c, qk, bias, vx):
        sc = qk + bias
        m_old = m_refs[c][...]
        m_new = jnp.maximum(m_old, jnp.max(sc, axis=-1, keepdims=True))
        alpha = jnp.exp2(m_old - m_new)
        pe = jnp.exp2(sc - jnp.concatenate([m_new] * (sc.shape[1] // LANES), axis=1))
        acc_refs[c][...] = alpha * acc_refs[c][...] + _dot(pe.astype(BF16), vx)
        m_refs[c][...] = m_new

    def slc_logits(kj, g, n=1):
        tiles = range(n)
        expand = jnp.concatenate([ex_ref[kj + j] for j in tiles], axis=1)
        unsel = (_dot(sel[g], expand) - 1.0) * (-NEG_INF)
        return _dot(qst[g], jnp.concatenate([kst_ref[0, g, kj + j] for j in tiles], axis=1)), unsel

    def slc_update(kj, g, logits, n=1):
        qk, unsel = logits
        off = pl.multiple_of(kj * TK, TK)
        heads = slice(g * NSA_GQA, (g + 1) * NSA_GQA)
        bias = jnp.concatenate([bt_ref[qi - kj - j, heads] for j in range(n)], axis=2)
        bias = (bias + unsel[None]).reshape(rows, n * TK)
        flash_step(g, qk, bias, vsx_ref[0, g, pl.ds(off, n * TK), :])

    def win_update(kj, g, qk):
        off = pl.multiple_of(kj * TK, TK)
        bias = bw_ref[qi - kj, g * NSA_GQA:(g + 1) * NSA_GQA].reshape(rows, TK)
        flash_step(NSA_KV_HEADS + g, qk, bias, vwx_ref[0, g, pl.ds(off, TK), :])

    lo = jnp.maximum(qi - (WIN_TILES - 1), 0)
    groups = range(NSA_KV_HEADS)

    def far_pair_body(i, carry):
        kj = 2 * i
        logits = [slc_logits(kj, g, 2) for g in groups]
        for g in groups:
            slc_update(kj, g, logits[g], 2)
        return carry

    def far_body(kj, carry):
        logits = [slc_logits(kj, g) for g in groups]
        for g in groups:
            slc_update(kj, g, logits[g])
        return carry

    n_pair = lo // 2
    lax.fori_loop(0, n_pair, far_pair_body, 0)
    lax.fori_loop(2 * n_pair, lo, far_body, 0)

    def near_body(kj, carry):
        s_logits = [slc_logits(kj, g) for g in groups]
        w_logits = [_dot(qst[g], kwt_ref[0, g, kj]) for g in groups]
        for g in groups:
            slc_update(kj, g, s_logits[g])
        for g in groups:
            win_update(kj, g, w_logits[g])
        return carry

    lax.fori_loop(lo, qi + 1, near_body, 0)

    def head_major(stacked):
        return jnp.concatenate([stacked[g][r * TQ:(r + 1) * TQ, :HEAD_DIM]
                                for g in groups for r in range(NSA_GQA)], axis=-1)

    def normalised(acc_ref):
        acc = acc_ref[...]
        return acc * (1.0 / pltpu.roll(acc, HEAD_DIM, axis=1))

    branches = (o_cmp,
                [normalised(acc_refs[g]) for g in groups],
                [normalised(acc_refs[NSA_KV_HEADS + g]) for g in groups])
    out = None
    for i, branch in enumerate(branches):
        term = _split_dot(gates, gx_ref[i], 3) * head_major(branch)
        out = term if out is None else out + term
    o_ref[...] = out


def _nsa_attention(proj, kc, vc, kst, vsx, kwt, vwx, bias_c, bias_t, bias_w, overlap, expand, gate_expand,
                   qgain, batch):
    nq = SEQ // TQ
    nhb = SEQ // CMP_STRIDE
    cmp_spec = pl.BlockSpec((1, NSA_KV_HEADS, nhb, HEAD_DIM), lambda b, i: (b, 0, 0, 0))
    kt_spec = pl.BlockSpec((1, NSA_KV_HEADS, N_KT, HEAD_DIM, TK), lambda b, i: (b, 0, 0, 0, 0))
    vx_spec = pl.BlockSpec((1, NSA_KV_HEADS, SEQ, LANES), lambda b, i: (b, 0, 0, 0))
    return pl.pallas_call(
        _nsa_kernel,
        grid=(batch, nq),
        in_specs=[pl.BlockSpec((TQ, GROUP_WIDTH), lambda b, i: (b * nq + i, COL_Q // GROUP_WIDTH)),
                  pl.BlockSpec((TQ, LANES), lambda b, i: (b * nq + i, COL_GATE // LANES)),
                  cmp_spec, cmp_spec, kt_spec, vx_spec, kt_spec, vx_spec,
                  pl.BlockSpec((N_HEADS, TQ, LANES), lambda b, i: (0, i, 0)),
                  pl.BlockSpec((N_KT, N_HEADS, TQ, TK), lambda b, i: (0, 0, 0, 0)),
                  pl.BlockSpec((WIN_TILES, N_HEADS, TQ, TK), lambda b, i: (0, 0, 0, 0)),
                  pl.BlockSpec((3 * LANES, LANES), lambda b, i: (0, 0)),
                  pl.BlockSpec((N_KT, LANES, TK), lambda b, i: (0, 0, 0)),
                  pl.BlockSpec((3, 3 * LANES, GROUP_WIDTH), lambda b, i: (0, 0, 0)),
                  pl.BlockSpec((1, HEAD_DIM), lambda b, i: (0, 0))],
        out_specs=pl.BlockSpec((TQ, GROUP_WIDTH), lambda b, i: (b * nq + i, 0)),
        out_shape=jax.ShapeDtypeStruct((batch * SEQ, GROUP_WIDTH), F32),
        scratch_shapes=[pltpu.VMEM((NSA_GQA * TQ, LANES), F32)] * (4 * NSA_KV_HEADS),
        compiler_params=_params("parallel", "arbitrary"),
        name="nsa_attention",
    )(proj, proj, kc, vc, kst, vsx, kwt, vwx, bias_c, bias_t, bias_w, overlap, expand, gate_expand, qgain)


def _stickbreak_kernel(q_ref, kt_ref, v_ref, tri_ref, o_ref, *scratch):
    acc_ref, tc_ref = scratch[:N_HEADS], scratch[N_HEADS:]
    qi = pl.program_id(1)
    q = q_ref[...] * (HEAD_DIM ** -0.5)
    qh = [q[:, h * HEAD_DIM:(h + 1) * HEAD_DIM].astype(BF16) for h in range(N_HEADS)]
    t = qi * TQ_SB + lax.broadcasted_iota(jnp.int32, (TQ_SB, TK), 0)
    lane = lax.broadcasted_iota(jnp.int32, (TQ_SB, TK), 1)
    tri = tri_ref[...]
    for h in range(N_HEADS):
        acc_ref[h][...] = jnp.zeros((TQ_SB, LANES), F32)
        tc_ref[h][...] = jnp.zeros((TQ_SB, TK), F32)

    def tile_step(kj, masked):
        off = pl.multiple_of(kj * TK, TK)
        heads = range(N_HEADS)
        if masked:
            ok = (lane + off) < t
        z = [_dot(qh[h], kt_ref[0, kj, h * HEAD_DIM:(h + 1) * HEAD_DIM, :]) for h in heads]
        log_beta, r = [], []
        for h in heads:
            soft = jnp.log(1.0 + jnp.exp2(jnp.abs(z[h]) * (-LOG2E)))
            lb = jnp.minimum(z[h], 0.0) - soft
            log_1m = lb - z[h]
            if masked:
                log_1m = jnp.where(ok, log_1m, 0.0)
            log_beta.append(lb)
            r.append(_split_dot(log_1m, tri, 2))
        for h in heads:
            vt = v_ref[pl.ds(off, TK), (h // 2) * LANES:(h // 2 + 1) * LANES]
            a = jnp.exp(log_beta[h] + r[h][:, :TK] + tc_ref[h][...])
            if masked:
                a = jnp.where(ok, a, 0.0)
            acc_ref[h][...] += _dot(a.astype(BF16), vt)
            tc_ref[h][...] += r[h][:, TK:]

    n_diag = TQ_SB // TK
    for d in range(n_diag):
        tile_step((qi + 1) * n_diag - 1 - d, True)

    def body(i, carry):
        tile_step(qi * n_diag - 1 - i, False)
        return carry

    lax.fori_loop(0, qi * n_diag, body, 0)

    lane_o = lax.broadcasted_iota(jnp.int32, (TQ_SB, LANES), 1)
    for pair in range(N_HEADS // 2):
        o_ref[:, pair * LANES:(pair + 1) * LANES] = jnp.where(
            lane_o < HEAD_DIM, acc_ref[2 * pair][...], acc_ref[2 * pair + 1][...])


def _stickbreak(proj, kt, v, tri, batch):
    nq = SEQ // TQ_SB
    return pl.pallas_call(
        _stickbreak_kernel,
        grid=(batch, nq),
        in_specs=[pl.BlockSpec((TQ_SB, GROUP_WIDTH), lambda b, i: (b * nq + i, COL_DQ // GROUP_WIDTH)),
                  pl.BlockSpec((1, N_KT, GROUP_WIDTH, TK), lambda b, i: (b, 0, 0, 0)),
                  pl.BlockSpec((SEQ, GROUP_WIDTH), lambda b, i: (b, 0)),
                  pl.BlockSpec((2 * TK, 2 * TK), lambda b, i: (0, 0))],
        out_specs=pl.BlockSpec((TQ_SB, GROUP_WIDTH), lambda b, i: (b * nq + i, 0)),
        out_shape=jax.ShapeDtypeStruct((batch * SEQ, GROUP_WIDTH), F32),
        scratch_shapes=[pltpu.VMEM((TQ_SB, LANES), F32)] * (2 * N_HEADS),
        compiler_params=_params("parallel", "arbitrary"),
        name="stickbreak",
    )(proj, kt, v, tri)


def _conv_sgu_kernel(bg_ref, cg_ref, hh_ref, cgp_ref, hhp_ref, cu_ref, cv_ref, cw_ref, sw_ref, sb_ref,
                     ob_ref, oc_ref):
    i = pl.program_id(0)
    z = cg_ref[...] * hh_ref[...]
    first = (i % (SEQ // TM_BC)) == 0
    zp = jnp.where(first, 0.0, cgp_ref[...] * hhp_ref[...])
    row = lax.broadcasted_iota(jnp.int32, z.shape, 0)
    p1 = jnp.broadcast_to(zp[7:8], z.shape)
    p2 = jnp.broadcast_to(zp[6:7], z.shape)
    z1 = jnp.where(row == 0, p1, pltpu.roll(z, 1, axis=0))
    z2 = jnp.where(row == 0, p2, jnp.where(row == 1, p1, pltpu.roll(z, 2, axis=0)))
    cw = cw_ref[...]
    y = cw[0:1] * z2
    y = y + cw[1:2] * z1
    y = y + cw[2:3] * z
    ob_ref[...] = bg_ref[...] * y

    u = _gelu(cu_ref[...])
    v = _gelu(cv_ref[...])
    mu = jnp.mean(v, axis=-1, keepdims=True)
    vc = v - mu
    var = jnp.mean(vc * vc, axis=-1, keepdims=True)
    vn = (vc * lax.rsqrt(var + 1e-5)).astype(BF16)
    pr = lax.broadcasted_iota(jnp.int32, (CHUNK, CHUNK), 0)
    pc = lax.broadcasted_iota(jnp.int32, (CHUNK, CHUNK), 1)
    ws = [jnp.where(pr >= pc, sw_ref[h], 0.0).astype(BF16) for h in range(N_HEADS)]
    for c in range(TM_BC // CHUNK):
        rs = slice(c * CHUNK, (c + 1) * CHUNK)
        s = jnp.concatenate(
            [_dot(ws[h], vn[rs, h * HEAD_DIM:(h + 1) * HEAD_DIM]) for h in range(N_HEADS)], axis=-1)
        oc_ref[rs, :] = u[rs] * (s + sb_ref[...])


def _conv_sgu(proj, conv_w, sgu_w, sgu_bx):
    m = proj.shape[0]
    gw = GROUP_WIDTH

    def col(c):
        return pl.BlockSpec((TM_BC, gw), lambda i: (i, c // gw))

    def prev(c):
        return pl.BlockSpec((8, gw), lambda i: (jnp.maximum(i * (TM_BC // 8) - 1, 0), c // gw))

    out = jax.ShapeDtypeStruct((m, gw), F32)
    return pl.pallas_call(
        _conv_sgu_kernel,
        grid=(m // TM_BC,),
        in_specs=[col(COL_BG), col(COL_BC), col(COL_BH), prev(COL_BC), prev(COL_BH), col(COL_CU), col(COL_CV),
                  pl.BlockSpec((CONV_W, gw), lambda i: (0, 0)),
                  pl.BlockSpec((N_HEADS, CHUNK, CHUNK), lambda i: (0, 0, 0)),
                  pl.BlockSpec((CHUNK, gw), lambda i: (0, 0))],
        out_specs=[pl.BlockSpec((TM_BC, gw), lambda i: (i, 0))] * 2,
        out_shape=[out, out],
        compiler_params=_params("parallel"),
        name="conv_sgu",
    )(proj, proj, proj, proj, proj, proj, proj, conv_w, sgu_w, sgu_bx)


def _outproj_kernel(oa_ref, ob_ref, oc_ref, od_ref, gg_ref, w_ref, x_ref, o_ref):
    parts = []
    for i, ref in enumerate((oa_ref, ob_ref, oc_ref, od_ref)):
        o = ref[...]
        ms = jnp.mean(o * o, axis=-1, keepdims=True)
        parts.append(((o * lax.rsqrt(ms + 1e-6)) * gg_ref[i:i + 1]).astype(BF16))
    o_ref[...] = x_ref[...] + _dot(jnp.concatenate(parts, axis=1), w_ref[...])


def _outproj(oa, ob, oc, od, gg, w, x2):
    m = x2.shape[0]
    mix_spec = pl.BlockSpec((TM_OUT, GROUP_WIDTH), lambda i: (i, 0))
    return pl.pallas_call(
        _outproj_kernel,
        grid=(m // TM_OUT,),
        in_specs=[mix_spec, mix_spec, mix_spec, mix_spec,
                  pl.BlockSpec((4, GROUP_WIDTH), lambda i: (0, 0)),
                  pl.BlockSpec((D_MODEL, D_MODEL), lambda i: (0, 0), pipeline_mode=pl.Buffered(1)),
                  pl.BlockSpec((TM_OUT, D_MODEL), lambda i: (i, 0))],
        out_specs=pl.BlockSpec((TM_OUT, D_MODEL), lambda i: (i, 0)),
        out_shape=jax.ShapeDtypeStruct((m, D_MODEL), F32),
        compiler_params=_params("parallel"),
        name="outproj",
    )(oa, ob, oc, od, gg, w, x2)


def _ffn_kernel(x_ref, g_ref, wg_ref, wu_ref, wd_ref, o_ref, h_ref):
    f = pl.program_id(1)

    @pl.when(f == 0)
    def _():
        x = x_ref[...]
        ms = jnp.mean(x * x, axis=-1, keepdims=True)
        h_ref[...] = ((x * lax.rsqrt(ms + 1e-6)) * g_ref[...]).astype(BF16)
        o_ref[...] = x

    h = h_ref[...]
    half = TF_FFN // 2
    pre = [(_dot(h, wg_ref[:, s:s + half]), _dot(h, wu_ref[:, s:s + half])) for s in (0, half)]
    down = None
    for s, (gate, up) in zip((0, half), pre):
        act = ((gate * jax.nn.sigmoid(gate)) * up).astype(BF16)
        d = _dot(act, wd_ref[s:s + half, :])
        down = d if down is None else down + d
    o_ref[...] += down


def _ffn(x2, gain, wg, wu, wd):
    m = x2.shape[0]
    return pl.pallas_call(
        _ffn_kernel,
        grid=(m // TM_FFN, D_FF // TF_FFN),
        in_specs=[pl.BlockSpec((TM_FFN, D_MODEL), lambda i, f: (i, 0)),
                  pl.BlockSpec((1, D_MODEL), lambda i, f: (0, 0)),
                  pl.BlockSpec((D_MODEL, TF_FFN), lambda i, f: (0, f)),
                  pl.BlockSpec((D_MODEL, TF_FFN), lambda i, f: (0, f)),
                  pl.BlockSpec((TF_FFN, D_MODEL), lambda i, f: (f, 0))],
        out_specs=pl.BlockSpec((TM_FFN, D_MODEL), lambda i, f: (i, 0)),
        out_shape=jax.ShapeDtypeStruct((m, D_MODEL), F32),
        scratch_shapes=[pltpu.VMEM((TM_FFN, D_MODEL), BF16)],
        compiler_params=_params("parallel", "arbitrary"),
        name="ffn",
    )(x2, gain, wg, wu, wd)


def _static_tables():
    i = np.arange(TQ)[:, None]
    j = np.arange(TK)[None, :]
    dist_t = np.arange(N_KT)[:, None, None] * TK + i[None] - j[None]
    bkt_t = np.where(dist_t >= 0, _rel_bucket_np(dist_t), MASKED_BUCKET)
    dist_w = dist_t[:WIN_TILES]
    bkt_w = np.where((dist_w >= 0) & (dist_w < WINDOW), _rel_bucket_np(dist_w), MASKED_BUCKET)
    t = np.arange(SEQ)[:, None]
    n = np.arange(LANES)[None, :]
    bkt_c = _rel_bucket_np(t - (n * CMP_STRIDE + CMP_LEN - 1))
    c0 = np.arange(LANES)[:, None] * CMP_STRIDE
    s0 = np.arange(LANES)[None, :] * SLC_LEN
    ov = np.minimum(c0 + CMP_LEN, s0 + SLC_LEN) - np.maximum(c0, s0)
    ov = np.maximum(ov, 0) / CMP_LEN
    ov[N_CMP:, :] = 0.0
    ov[:, N_SLC:] = 0.0
    key_blk = (np.arange(N_KT)[:, None] * TK + np.arange(TK)[None, :]) // SLC_LEN
    expand = (np.arange(LANES)[None, :, None] == key_blk[:, None, :]).astype(np.float32)
    tri = (np.arange(TK)[:, None] > np.arange(TK)[None, :]).astype(np.float32)
    tri = np.concatenate([tri, np.ones((TK, TK), np.float32)], axis=1)
    gate_expand = np.zeros((3, LANES, GROUP_WIDTH), np.float32)
    for i in range(3):
        for h in range(N_HEADS):
            gate_expand[i, 3 * h + i, h * HEAD_DIM:(h + 1) * HEAD_DIM] = 1.0
    return (bkt_t.reshape(-1, TK).astype(np.int32), bkt_w.reshape(-1, TK).astype(np.int32), bkt_c,
            ov.astype(np.float32), expand, tri, gate_expand)


def _permute_w_in(w_in):
    depth = w_in.shape[0]
    w = w_in.astype(BF16)
    pad = jnp.zeros((depth, D_MODEL, PROJ_COLS - W_IN_COLS), BF16)
    gate_pad, tail_pad = pad[..., :LANES - 3 * N_HEADS], pad[..., LANES - 3 * N_HEADS:]
    return jnp.concatenate([w[..., 0:512], w[..., 1304:5400], w[..., 512:1280],
                            w[..., 1280:1304], gate_pad, tail_pad], axis=-1)


def kernel(x, w_in, w_out, norm_mix, norm_ffn, q_gain, k_gain, cmp_pos, cmp_w1, cmp_w2, rel_table, conv_w,
           sgu_w, sgu_b, group_gain, w_ffn_gate, w_ffn_up, w_ffn_down):
    batch, seq, d_model = x.shape
    assert seq == SEQ and d_model == D_MODEL
    depth = w_in.shape[0]
    m = batch * seq
    nhb = SEQ // CMP_STRIDE

    bkt_t, bkt_w, bkt_c, ov_np, expand_np, tri_np, gx_np = _static_tables()

    def tiles(b):
        return b.reshape(N_HEADS, -1, TQ, TK).transpose(1, 0, 2, 3)

    bias_t = tiles(_bias_expand(rel_table, bkt_t))
    bias_w = tiles(_bias_expand(rel_table, bkt_w))
    bias_c = _bias_expand(rel_table, bkt_c)
    overlap = jnp.asarray(_stack_rows(ov_np, 3), BF16)
    expand = jnp.asarray(expand_np, BF16)
    tri = jnp.asarray(_stack_rows(tri_np, 2), BF16)
    gate_expand = jnp.asarray(np.concatenate([gx_np] * 3, axis=1), BF16)

    w_in_p = _permute_w_in(w_in)
    w_out_b = w_out.astype(BF16)
    wg_b = w_ffn_gate.astype(BF16)
    wu_b = w_ffn_up.astype(BF16)
    wd_b = w_ffn_down.astype(BF16)
    w1_b = cmp_w1.astype(BF16)
    w2_b = cmp_w2.astype(BF16)
    pos = cmp_pos.reshape(depth, 2, 1, CMP_LEN * HEAD_DIM)
    kgain2 = jnp.tile(k_gain, (1, 2))
    sgu_bx = jnp.repeat(jnp.swapaxes(sgu_b, 1, 2), HEAD_DIM, axis=2)

    x2 = x.reshape(m, D_MODEL)
    for l in range(depth):
        proj = _inproj(x2, norm_mix[l][None], w_in_p[l])
        w1_blk, w2_blk = _compress_weights(cmp_w1[l], cmp_w2[l])
        kc, vc, kst, vsx, kwt, vwx, dk_t, dv = _nsa_prep(proj, w1_blk, w2_blk, _posterm(pos[l], w1_b[l]),
                                                         kgain2[l][None], batch)
        o_a = _nsa_attention(proj, kc, vc, kst, vsx, kwt, vwx,
                             bias_c, bias_t, bias_w, overlap, expand, gate_expand, q_gain[l][None], batch)
        o_d = _stickbreak(proj, dk_t, dv, tri, batch)
        o_b, o_c = _conv_sgu(proj, conv_w[l], sgu_w[l], sgu_bx[l])
        x2 = _outproj(o_a, o_b, o_c, o_d, group_gain[l].reshape(4, GROUP_WIDTH), w_out_b[l], x2)
        x2 = _ffn(x2, norm_ffn[l][None], wg_b[l], wu_b[l], wd_b[l])
    return x2.reshape(batch, seq, d_model)
```

```python
import math

import numpy as np
import jax
import jax.numpy as jnp
from jax import lax
from jax.experimental import pallas as pl
from jax.experimental.pallas import tpu as pltpu

F32 = jnp.float32
BF16 = jnp.bfloat16

D_MODEL = 2048
SEQ = 2048
HEAD_DIM = 64
GROUP_WIDTH = 512
N_HEADS = 8
NSA_KV_HEADS = 2
NSA_GQA = 4
CMP_LEN = 32
CMP_STRIDE = 16
SLC_LEN = 64
SLC_TOP = 16
WINDOW = 512
FORCE_BONUS = 1.0e3
N_BUCKETS = 32
MAX_DISTANCE = 1024
CONV_W = 3
CHUNK = 128
D_FF = 5632
NEG_INF = -1.0e30
N_CMP = (SEQ - CMP_LEN) // CMP_STRIDE + 1
N_SLC = SEQ // SLC_LEN
W_IN_COLS = 5400

LANES = 128
VMEM_LIMIT = 56 * 1024 * 1024

COL_Q = 0
COL_BG, COL_BC, COL_BH = 512, 1024, 1536
COL_CU, COL_CV = 2048, 2560
COL_DQ, COL_DK, COL_DV = 3072, 3584, 4096
COL_KV = 4608
COL_GATE = 5376
PROJ_COLS = 5632

TM_IN, TN_IN = 1024, 1408
TQ = 128
TK = 128
N_KT = SEQ // TK
WIN_TILES = WINDOW // TK + 1
TQ_SB = 256
TM_BC = 256
TM_OUT = 512
TM_FFN, TF_FFN = 1024, 512
MASKED_BUCKET = N_BUCKETS
LOG2E = math.log2(math.e)


def _params(*sem):
    return pltpu.CompilerParams(dimension_semantics=sem, vmem_limit_bytes=VMEM_LIMIT)


def _dot(a, b):
    return jnp.dot(a, b, preferred_element_type=F32)


def _dot_nt(a, b):
    return lax.dot_general(a, b, (((1,), (1,)), ((), ())), preferred_element_type=F32)


def _split_dot(a, b_stacked, terms):
    parts = []
    rem = a
    for i in range(terms):
        part = rem.astype(BF16)
        parts.append(part)
        if i + 1 < terms:
            rem = rem - part.astype(F32)
    return _dot(jnp.concatenate(parts, axis=1), b_stacked)


def _stack_rows(b_np, terms):
    return np.concatenate([b_np] * terms, axis=0)


def _gelu(x):
    c = math.sqrt(2.0 / math.pi)
    return x * (0.5 * (1.0 + jnp.tanh(c * (x + 0.044715 * (x * x * x)))))


def _rel_bucket_np(dist):
    n = np.maximum(dist, 0)
    max_exact = N_BUCKETS // 2
    nf = np.maximum(n, 1).astype(np.float64)
    large = max_exact + (np.log(nf / max_exact) / math.log(MAX_DISTANCE / max_exact)
                         * (N_BUCKETS - max_exact)).astype(np.int32)
    large = np.minimum(large, N_BUCKETS - 1)
    return np.where(n < max_exact, n, large).astype(np.int32)


def _bias_expand_kernel(tab_ref, bkt_ref, o_ref):
    bkt = bkt_ref[...]
    for h in range(N_HEADS):
        acc = jnp.full(bkt.shape, NEG_INF, F32)
        for b in range(N_BUCKETS):
            acc = jnp.where(bkt == b, tab_ref[b, h] * LOG2E, acc)
        o_ref[h] = acc


def _bias_expand(rel_table, bucket_np):
    rows = bucket_np.shape[0]
    tr = 128
    return pl.pallas_call(
        _bias_expand_kernel,
        grid=(rows // tr,),
        in_specs=[pl.BlockSpec(memory_space=pltpu.SMEM),
                  pl.BlockSpec((tr, LANES), lambda i: (i, 0))],
        out_specs=pl.BlockSpec((N_HEADS, tr, LANES), lambda i: (0, i, 0)),
        out_shape=jax.ShapeDtypeStruct((N_HEADS, rows, LANES), F32),
        compiler_params=_params("parallel"),
        name="bias_expand",
    )(rel_table, jnp.asarray(bucket_np))


def _inproj_kernel(x_ref, g_ref, w_ref, o_ref, h_ref):
    @pl.when(pl.program_id(1) == 0)
    def _():
        x = x_ref[...]
        ms = jnp.mean(x * x, axis=-1, keepdims=True)
        h_ref[...] = ((x * lax.rsqrt(ms + 1e-6)) * g_ref[...]).astype(BF16)

    o_ref[...] = _dot(h_ref[...], w_ref[...])


def _inproj(x2, gain, w):
    m = x2.shape[0]
    return pl.pallas_call(
        _inproj_kernel,
        grid=(m // TM_IN, PROJ_COLS // TN_IN),
        in_specs=[pl.BlockSpec((TM_IN, D_MODEL), lambda i, j: (i, 0)),
                  pl.BlockSpec((1, D_MODEL), lambda i, j: (0, 0)),
                  pl.BlockSpec((D_MODEL, TN_IN), lambda i, j: (0, j))],
        out_specs=pl.BlockSpec((TM_IN, TN_IN), lambda i, j: (i, j)),
        out_shape=jax.ShapeDtypeStruct((m, PROJ_COLS), F32),
        scratch_shapes=[pltpu.VMEM((TM_IN, D_MODEL), BF16)],
        compiler_params=_params("parallel", "arbitrary"),
        name="inproj",
    )(x2, gain, w)


def _seg_rms(x, gain2):
    lane = lax.broadcasted_iota(jnp.int32, x.shape, 1)
    x2 = x * x
    lo = lane < HEAD_DIM
    s0 = jnp.sum(jnp.where(lo, x2, 0.0), axis=-1, keepdims=True)
    s1 = jnp.sum(jnp.where(lo, 0.0, x2), axis=-1, keepdims=True)
    ms = jnp.where(lo, s0, s1) * (1.0 / HEAD_DIM)
    return (x * lax.rsqrt(ms + 1e-6)) * gain2


def _nsa_prep_kernel(kv_ref, kcin_ref, vcin_ref, dk_ref, dv_ref, w1_ref, w2_ref, posterm_ref, kg2_ref,
                     kc_ref, vc_ref, kst_ref, vsx_ref, kwt_ref, vwx_ref, dkt_ref, dvb_ref):
    kg2 = kg2_ref[...]
    lane = lax.broadcasted_iota(jnp.int32, (SEQ, LANES), 1)
    for col, kt_ref, vx_ref in ((2, kst_ref, vsx_ref), (4, kwt_ref, vwx_ref)):
        kn = _seg_rms(kv_ref[:, col * LANES:(col + 1) * LANES], kg2)
        for kj in range(N_KT):
            tile_t = kn[kj * TK:(kj + 1) * TK].T
            for g in range(NSA_KV_HEADS):
                kt_ref[0, g, kj] = tile_t[g * HEAD_DIM:(g + 1) * HEAD_DIM].astype(BF16)
        v = kv_ref[:, (col + 1) * LANES:(col + 2) * LANES]
        vx_ref[0, 0] = jnp.where(lane < HEAD_DIM, v, 1.0).astype(BF16)
        vx_ref[0, 1] = jnp.where(lane < HEAD_DIM, pltpu.roll(v, HEAD_DIM, axis=1), 1.0).astype(BF16)

    for kj in range(N_KT):
        dkt_ref[0, kj] = dk_ref[kj * TK:(kj + 1) * TK, :].T.astype(BF16)
    dvb_ref[...] = dv_ref[...].astype(BF16)

    nhb = SEQ // CMP_STRIDE
    first = None
    second = None
    for l in range(CMP_STRIDE):
        rows_l = pl.ds(l, nhb, stride=CMP_STRIDE)
        x_l = jnp.concatenate([kcin_ref[rows_l, :], vcin_ref[rows_l, :]], axis=1).astype(BF16)
        a = _dot(x_l, w1_ref[0, l])
        b = _dot(x_l, w1_ref[1, l])
        first = a if first is None else first + a
        second = b if second is None else second + b
    pre = first + pltpu.roll(second, nhb - 1, axis=0) + posterm_ref[...]
    out = _dot(_gelu(pre).astype(BF16), w2_ref[...])
    keys = _seg_rms(out[:, :LANES], kg2).astype(BF16)
    vals = out[:, LANES:].astype(BF16)
    for g in range(NSA_KV_HEADS):
        kc_ref[0, g] = keys[:, g * HEAD_DIM:(g + 1) * HEAD_DIM]
        vc_ref[0, g] = vals[:, g * HEAD_DIM:(g + 1) * HEAD_DIM]


def _compress_weights(w1, w2):
    n_stream = 2 * NSA_KV_HEADS
    eye = jnp.eye(n_stream, dtype=w1.dtype)
    w1r = w1.reshape(2, 2, CMP_STRIDE, HEAD_DIM, HEAD_DIM).transpose(1, 2, 0, 3, 4)
    w1s = jnp.repeat(w1r, NSA_KV_HEADS, axis=2)
    w1_blk = jnp.einsum('hlcde,cf->hlcdfe', w1s, eye).reshape(2, CMP_STRIDE, n_stream * HEAD_DIM, n_stream * HEAD_DIM)
    w2s = jnp.repeat(w2, NSA_KV_HEADS, axis=0)
    w2_blk = jnp.einsum('cde,cf->cdfe', w2s, eye).reshape(n_stream * HEAD_DIM, n_stream * HEAD_DIM)
    return w1_blk.astype(BF16), w2_blk.astype(BF16)


def _posterm_kernel(pos_ref, w1_ref, o_ref):
    for i in range(2):
        pos = jnp.broadcast_to(pos_ref[i], (8, CMP_LEN * HEAD_DIM)).astype(BF16)
        term = _dot(pos, w1_ref[i])
        o_ref[:, i * LANES:(i + 1) * LANES] = jnp.concatenate([term] * NSA_KV_HEADS, axis=1)


def _posterm(pos, w1):
    return pl.pallas_call(
        _posterm_kernel,
        out_shape=jax.ShapeDtypeStruct((8, 2 * NSA_KV_HEADS * HEAD_DIM), F32),
        name="cmp_posterm",
    )(pos, w1)[0:1]


def _nsa_prep(proj, w1_blk, w2_blk, posterm, kgain2, batch):
    nhb = SEQ // CMP_STRIDE
    cmp_shape = jax.ShapeDtypeStruct((batch, NSA_KV_HEADS, nhb, HEAD_DIM), BF16)
    cmp_spec = pl.BlockSpec((1, NSA_KV_HEADS, nhb, HEAD_DIM), lambda b: (b, 0, 0, 0))
    kt_shape = jax.ShapeDtypeStruct((batch, NSA_KV_HEADS, N_KT, HEAD_DIM, TK), BF16)
    kt_spec = pl.BlockSpec((1, NSA_KV_HEADS, N_KT, HEAD_DIM, TK), lambda b: (b, 0, 0, 0, 0))
    vx_shape = jax.ShapeDtypeStruct((batch, NSA_KV_HEADS, SEQ, LANES), BF16)
    vx_spec = pl.BlockSpec((1, NSA_KV_HEADS, SEQ, LANES), lambda b: (b, 0, 0, 0))
    dkt_shape = jax.ShapeDtypeStruct((batch, N_KT, GROUP_WIDTH, TK), BF16)
    dkt_spec = pl.BlockSpec((1, N_KT, GROUP_WIDTH, TK), lambda b: (b, 0, 0, 0))
    dvb_shape = jax.ShapeDtypeStruct((batch * SEQ, GROUP_WIDTH), BF16)
    dvb_spec = pl.BlockSpec((SEQ, GROUP_WIDTH), lambda b: (b, 0))
    return pl.pallas_call(
        _nsa_prep_kernel,
        grid=(batch,),
        in_specs=[pl.BlockSpec((SEQ, 6 * LANES), lambda b: (b, COL_KV // (6 * LANES))),
                  pl.BlockSpec((SEQ, LANES), lambda b: (b, COL_KV // LANES)),
                  pl.BlockSpec((SEQ, LANES), lambda b: (b, COL_KV // LANES + 1)),
                  pl.BlockSpec((SEQ, GROUP_WIDTH), lambda b: (b, COL_DK // GROUP_WIDTH)),
                  pl.BlockSpec((SEQ, GROUP_WIDTH), lambda b: (b, COL_DV // GROUP_WIDTH)),
                  pl.BlockSpec((2, CMP_STRIDE, 2 * LANES, 2 * LANES), lambda b: (0, 0, 0, 0)),
                  pl.BlockSpec((2 * LANES, 2 * LANES), lambda b: (0, 0)),
                  pl.BlockSpec((1, 2 * LANES), lambda b: (0, 0)),
                  pl.BlockSpec((1, LANES), lambda b: (0, 0))],
        out_specs=[cmp_spec, cmp_spec, kt_spec, vx_spec, kt_spec, vx_spec, dkt_spec, dvb_spec],
        out_shape=[cmp_shape, cmp_shape, kt_shape, vx_shape, kt_shape, vx_shape, dkt_shape, dvb_shape],
        compiler_params=_params("parallel"),
        name="nsa_prep",
    )(proj, proj, proj, proj, proj, w1_blk, w2_blk, posterm, kgain2)


def _nsa_kernel(q_ref, gate_ref, kc_ref, vc_ref, kst_ref, vsx_ref, kwt_ref, vwx_ref,
                bc_ref, bt_ref, bw_ref, ov_ref, ex_ref, gx_ref, qg_ref, o_ref, *scratch):
    n_chain = 2 * NSA_KV_HEADS
    m_refs, acc_refs = scratch[:n_chain], scratch[n_chain:]
    qi = pl.program_id(1)
    rows = NSA_GQA * TQ
    q = q_ref[...]
    qg = qg_ref[...] * (HEAD_DIM ** -0.5 * LOG2E)
    qn = []
    for h in range(N_HEADS):
        seg = q[:, h * HEAD_DIM:(h + 1) * HEAD_DIM]
        ms = jnp.mean(seg * seg, axis=-1, keepdims=True)
        qn.append(((seg * lax.rsqrt(ms + 1e-6)) * qg).astype(BF16))
    qst = [jnp.concatenate(qn[g * NSA_GQA:(g + 1) * NSA_GQA], axis=0) for g in range(NSA_KV_HEADS)]
    gates = jax.nn.sigmoid(gate_ref[...])

    t1 = qi * TQ + lax.broadcasted_iota(jnp.int32, (TQ, LANES), 0)
    t4 = jnp.concatenate([t1] * NSA_GQA, axis=0)
    lane1 = lax.broadcasted_iota(jnp.int32, (TQ, LANES), 1)
    lane4 = lax.broadcasted_iota(jnp.int32, (rows, LANES), 1)
    blk_t = lax.broadcasted_iota(jnp.int32, (N_SLC, TQ), 0)

    o_cmp, sel = [], []
    for g in range(NSA_KV_HEADS):
        s = _dot_nt(qst[g], kc_ref[0, g]) + bc_ref[g * NSA_GQA:(g + 1) * NSA_GQA].reshape(rows, LANES)
        mask_c = (t4 - (lane4 * CMP_STRIDE + (CMP_LEN - 1)) >= 0) & (lane4 < N_CMP)
        s = jnp.where(mask_c, s, NEG_INF)
        m = jnp.max(s, axis=-1, keepdims=True)
        e = jnp.where(mask_c, jnp.exp2(s - m), 0.0)
        l = jnp.sum(e, axis=-1, keepdims=True)
        p = e * jnp.where(l > 0.0, 1.0 / l, 0.0)
        o_cmp.append(_dot(p.astype(BF16), vc_ref[0, g]))

        psum = p[0:TQ] + p[TQ:2 * TQ] + p[2 * TQ:3 * TQ] + p[3 * TQ:4 * TQ]
        imp = _split_dot(psum, ov_ref[...], 3)
        cur = t1 >> 6
        valid = (lane1 * SLC_LEN <= t1) & (lane1 < N_SLC)
        forced = (lane1 == 0) | (lane1 == cur) | (lane1 == cur - 1)
        score = jnp.where(valid, imp + jnp.where(forced, FORCE_BONUS, 0.0), NEG_INF)
        st = score.T[0:N_SLC]
        rank = jnp.zeros((N_SLC, TQ), F32)
        for k in range(N_SLC):
            ck = st[k:k + 1, :]
            before = (ck > st) | ((ck == st) & (blk_t > k))
            rank = rank + jnp.where(before, 1.0, 0.0)
        sel_t = jnp.where(rank < float(SLC_TOP), 1.0, 0.0)
        sel_t = jnp.concatenate([sel_t, jnp.zeros((LANES - N_SLC, TQ), F32)], axis=0)
        sel.append(sel_t.T.astype(BF16))

    for c in range(n_chain):
        m_refs[c][...] = jnp.full((rows, LANES), NEG_INF, F32)
        acc_refs[c][...] = jnp.zeros((rows, LANES), F32)

    def flash_step(c, qk, bias, vx):
        sc = qk + bias
        m_old = m_refs[c][...]
        m_new = jnp.maximum(m_old, jnp.max(sc, axis=-1, keepdims=True))
        alpha = jnp.exp2(m_old - m_new)
        pe = jnp.exp2(sc - jnp.concatenate([m_new] * (sc.shape[1] // LANES), axis=1))
        acc_refs[c][...] = alpha * acc_refs[c][...] + _dot(pe.astype(BF16), vx)
        m_refs[c][...] = m_new

    def slc_logits(kj, g, n=1):
        tiles = range(n)
        expand = jnp.concatenate([ex_ref[kj + j] for j in tiles], axis=1)
        unsel = (_dot(sel[g], expand) - 1.0) * (-NEG_INF)
        return _dot(qst[g], jnp.concatenate([kst_ref[0, g, kj + j] for j in tiles], axis=1)), unsel

    def slc_update(kj, g, logits, n=1):
        qk, unsel = logits
        off = pl.multiple_of(kj * TK, TK)
        heads = slice(g * NSA_GQA, (g + 1) * NSA_GQA)
        bias = jnp.concatenate([bt_ref[qi - kj - j, heads] for j in range(n)], axis=2)
        bias = (bias + unsel[None]).reshape(rows, n * TK)
        flash_step(g, qk, bias, vsx_ref[0, g, pl.ds(off, n * TK), :])

    def win_update(kj, g, qk):
        off = pl.multiple_of(kj * TK, TK)
        bias = bw_ref[qi - kj, g * NSA_GQA:(g + 1) * NSA_GQA].reshape(rows, TK)
        flash_step(NSA_KV_HEADS + g, qk, bias, vwx_ref[0, g, pl.ds(off, TK), :])

    lo = jnp.maximum(qi - (WIN_TILES - 1), 0)
    groups = range(NSA_KV_HEADS)

    def far_pair_body(i, carry):
        kj = 2 * i
        logits = [slc_logits(kj, g, 2) for g in groups]
        for g in groups:
            slc_update(kj, g, logits[g], 2)
        return carry

    def far_body(kj, carry):
        logits = [slc_logits(kj, g) for g in groups]
        for g in groups:
            slc_update(kj, g, logits[g])
        return carry

    n_pair = lo // 2
    lax.fori_loop(0, n_pair, far_pair_body, 0)
    lax.fori_loop(2 * n_pair, lo, far_body, 0)

    def near_body(kj, carry):
        s_logits = [slc_logits(kj, g) for g in groups]
        w_logits = [_dot(qst[g], kwt_ref[0, g, kj]) for g in groups]
        for g in groups:
            slc_update(kj, g, s_logits[g])
        for g in groups:
            win_update(kj, g, w_logits[g])
        return carry

    lax.fori_loop(lo, qi + 1, near_body, 0)

    def head_major(stacked):
        return jnp.concatenate([stacked[g][r * TQ:(r + 1) * TQ, :HEAD_DIM]
                                for g in groups for r in range(NSA_GQA)], axis=-1)

    def normalised(acc_ref):
        acc = acc_ref[...]
        return acc * (1.0 / pltpu.roll(acc, HEAD_DIM, axis=1))

    branches = (o_cmp,
                [normalised(acc_refs[g]) for g in groups],
                [normalised(acc_refs[NSA_KV_HEADS + g]) for g in groups])
    out = None
    for i, branch in enumerate(branches):
        term = _split_dot(gates, gx_ref[i], 3) * head_major(branch)
        out = term if out is None else out + term
    o_ref[...] = out


def _nsa_attention(proj, kc, vc, kst, vsx, kwt, vwx, bias_c, bias_t, bias_w, overlap, expand, gate_expand,
                   qgain, batch):
    nq = SEQ // TQ
    nhb = SEQ // CMP_STRIDE
    cmp_spec = pl.BlockSpec((1, NSA_KV_HEADS, nhb, HEAD_DIM), lambda b, i: (b, 0, 0, 0))
    kt_spec = pl.BlockSpec((1, NSA_KV_HEADS, N_KT, HEAD_DIM, TK), lambda b, i: (b, 0, 0, 0, 0))
    vx_spec = pl.BlockSpec((1, NSA_KV_HEADS, SEQ, LANES), lambda b, i: (b, 0, 0, 0))
    return pl.pallas_call(
        _nsa_kernel,
        grid=(batch, nq),
        in_specs=[pl.BlockSpec((TQ, GROUP_WIDTH), lambda b, i: (b * nq + i, COL_Q // GROUP_WIDTH)),
                  pl.BlockSpec((TQ, LANES), lambda b, i: (b * nq + i, COL_GATE // LANES)),
                  cmp_spec, cmp_spec, kt_spec, vx_spec, kt_spec, vx_spec,
                  pl.BlockSpec((N_HEADS, TQ, LANES), lambda b, i: (0, i, 0)),
                  pl.BlockSpec((N_KT, N_HEADS, TQ, TK), lambda b, i: (0, 0, 0, 0)),
                  pl.BlockSpec((WIN_TILES, N_HEADS, TQ, TK), lambda b, i: (0, 0, 0, 0)),
                  pl.BlockSpec((3 * LANES, LANES), lambda b, i: (0, 0)),
                  pl.BlockSpec((N_KT, LANES, TK), lambda b, i: (0, 0, 0)),
                  pl.BlockSpec((3, 3 * LANES, GROUP_WIDTH), lambda b, i: (0, 0, 0)),
                  pl.BlockSpec((1, HEAD_DIM), lambda b, i: (0, 0))],
        out_specs=pl.BlockSpec((TQ, GROUP_WIDTH), lambda b, i: (b * nq + i, 0)),
        out_shape=jax.ShapeDtypeStruct((batch * SEQ, GROUP_WIDTH), F32),
        scratch_shapes=[pltpu.VMEM((NSA_GQA * TQ, LANES), F32)] * (4 * NSA_KV_HEADS),
        compiler_params=_params("parallel", "arbitrary"),
        name="nsa_attention",
    )(proj, proj, kc, vc, kst, vsx, kwt, vwx, bias_c, bias_t, bias_w, overlap, expand, gate_expand, qgain)


def _stickbreak_kernel(q_ref, kt_ref, v_ref, tri_ref, o_ref, *scratch):
    acc_ref, tc_ref = scratch[:N_HEADS], scratch[N_HEADS:]
    qi = pl.program_id(1)
    q = q_ref[...] * (HEAD_DIM ** -0.5)
    qh = [q[:, h * HEAD_DIM:(h + 1) * HEAD_DIM].astype(BF16) for h in range(N_HEADS)]
    t = qi * TQ_SB + lax.broadcasted_iota(jnp.int32, (TQ_SB, TK), 0)
    lane = lax.broadcasted_iota(jnp.int32, (TQ_SB, TK), 1)
    tri = tri_ref[...]
    for h in range(N_HEADS):
        acc_ref[h][...] = jnp.zeros((TQ_SB, LANES), F32)
        tc_ref[h][...] = jnp.zeros((TQ_SB, TK), F32)

    def tile_step(kj, masked, row0=0):
        off = pl.multiple_of(kj * TK, TK)
        heads = range(N_HEADS)
        rs = slice(row0, TQ_SB)
        if masked:
            ok = ((lane + off) < t)[rs]
        z = [_dot(qh[h][rs], kt_ref[0, kj, h * HEAD_DIM:(h + 1) * HEAD_DIM, :]) for h in heads]
        log_beta, r = [], []
        for h in heads:
            soft = jnp.log(1.0 + jnp.exp2(jnp.abs(z[h]) * (-LOG2E)))
            lb = jnp.minimum(z[h], 0.0) - soft
            log_1m = lb - z[h]
            if masked:
                log_1m = jnp.where(ok, log_1m, 0.0)
            log_beta.append(lb)
            r.append(_split_dot(log_1m, tri, 2))
        for h in heads:
            vt = v_ref[pl.ds(off, TK), (h // 2) * LANES:(h // 2 + 1) * LANES]
            a = jnp.exp(log_beta[h] + r[h][:, :TK] + tc_ref[h][rs, :])
            if masked:
                a = jnp.where(ok, a, 0.0)
            acc_ref[h][rs, :] += _dot(a.astype(BF16), vt)
            tc_ref[h][rs, :] += r[h][:, TK:]

    n_diag = TQ_SB // TK
    for d in range(n_diag):
        tile_step((qi + 1) * n_diag - 1 - d, True, (n_diag - 1 - d) * TK)

    def body(i, carry):
        tile_step(qi * n_diag - 1 - i, False)
        return carry

    lax.fori_loop(0, qi * n_diag, body, 0)

    lane_o = lax.broadcasted_iota(jnp.int32, (TQ_SB, LANES), 1)
    for pair in range(N_HEADS // 2):
        o_ref[:, pair * LANES:(pair + 1) * LANES] = jnp.where(
            lane_o < HEAD_DIM, acc_ref[2 * pair][...], acc_ref[2 * pair + 1][...])


def _stickbreak(proj, kt, v, tri, batch):
    nq = SEQ // TQ_SB
    return pl.pallas_call(
        _stickbreak_kernel,
        grid=(batch, nq),
        in_specs=[pl.BlockSpec((TQ_SB, GROUP_WIDTH), lambda b, i: (b * nq + i, COL_DQ // GROUP_WIDTH)),
                  pl.BlockSpec((1, N_KT, GROUP_WIDTH, TK), lambda b, i: (b, 0, 0, 0)),
                  pl.BlockSpec((SEQ, GROUP_WIDTH), lambda b, i: (b, 0)),
                  pl.BlockSpec((2 * TK, 2 * TK), lambda b, i: (0, 0))],
        out_specs=pl.BlockSpec((TQ_SB, GROUP_WIDTH), lambda b, i: (b * nq + i, 0)),
        out_shape=jax.ShapeDtypeStruct((batch * SEQ, GROUP_WIDTH), F32),
        scratch_shapes=[pltpu.VMEM((TQ_SB, LANES), F32)] * (2 * N_HEADS),
        compiler_params=_params("parallel", "arbitrary"),
        name="stickbreak",
    )(proj, kt, v, tri)


def _conv_sgu_kernel(bg_ref, cg_ref, hh_ref, cgp_ref, hhp_ref, cu_ref, cv_ref, cw_ref, sw_ref, sb_ref,
                     ob_ref, oc_ref):
    i = pl.program_id(0)
    z = cg_ref[...] * hh_ref[...]
    first = (i % (SEQ // TM_BC)) == 0
    zp = jnp.where(first, 0.0, cgp_ref[...] * hhp_ref[...])
    row = lax.broadcasted_iota(jnp.int32, z.shape, 0)
    p1 = jnp.broadcast_to(zp[7:8], z.shape)
    p2 = jnp.broadcast_to(zp[6:7], z.shape)
    z1 = jnp.where(row == 0, p1, pltpu.roll(z, 1, axis=0))
    z2 = jnp.where(row == 0, p2, jnp.where(row == 1, p1, pltpu.roll(z, 2, axis=0)))
    cw = cw_ref[...]
    y = cw[0:1] * z2
    y = y + cw[1:2] * z1
    y = y + cw[2:3] * z
    ob_ref[...] = bg_ref[...] * y

    u = _gelu(cu_ref[...])
    v = _gelu(cv_ref[...])
    mu = jnp.mean(v, axis=-1, keepdims=True)
    vc = v - mu
    var = jnp.mean(vc * vc, axis=-1, keepdims=True)
    vn = (vc * lax.rsqrt(var + 1e-5)).astype(BF16)
    pr = lax.broadcasted_iota(jnp.int32, (CHUNK, CHUNK), 0)
    pc = lax.broadcasted_iota(jnp.int32, (CHUNK, CHUNK), 1)
    ws = [jnp.where(pr >= pc, sw_ref[h], 0.0).astype(BF16) for h in range(N_HEADS)]
    for c in range(TM_BC // CHUNK):
        rs = slice(c * CHUNK, (c + 1) * CHUNK)
        s = jnp.concatenate(
            [_dot(ws[h], vn[rs, h * HEAD_DIM:(h + 1) * HEAD_DIM]) for h in range(N_HEADS)], axis=-1)
        oc_ref[rs, :] = u[rs] * (s + sb_ref[...])


def _conv_sgu(proj, conv_w, sgu_w, sgu_bx):
    m = proj.shape[0]
    gw = GROUP_WIDTH

    def col(c):
        return pl.BlockSpec((TM_BC, gw), lambda i: (i, c // gw))

    def prev(c):
        return pl.BlockSpec((8, gw), lambda i: (jnp.maximum(i * (TM_BC // 8) - 1, 0), c // gw))

    out = jax.ShapeDtypeStruct((m, gw), F32)
    return pl.pallas_call(
        _conv_sgu_kernel,
        grid=(m // TM_BC,),
        in_specs=[col(COL_BG), col(COL_BC), col(COL_BH), prev(COL_BC), prev(COL_BH), col(COL_CU), col(COL_CV),
                  pl.BlockSpec((CONV_W, gw), lambda i: (0, 0)),
                  pl.BlockSpec((N_HEADS, CHUNK, CHUNK), lambda i: (0, 0, 0)),
                  pl.BlockSpec((CHUNK, gw), lambda i: (0, 0))],
        out_specs=[pl.BlockSpec((TM_BC, gw), lambda i: (i, 0))] * 2,
        out_shape=[out, out],
        compiler_params=_params("parallel"),
        name="conv_sgu",
    )(proj, proj, proj, proj, proj, proj, proj, conv_w, sgu_w, sgu_bx)


def _outproj_kernel(oa_ref, ob_ref, oc_ref, od_ref, gg_ref, w_ref, x_ref, o_ref):
    parts = []
    for i, ref in enumerate((oa_ref, ob_ref, oc_ref, od_ref)):
        o = ref[...]
        ms = jnp.mean(o * o, axis=-1, keepdims=True)
        parts.append(((o * lax.rsqrt(ms + 1e-6)) * gg_ref[i:i + 1]).astype(BF16))
    o_ref[...] = x_ref[...] + _dot(jnp.concatenate(parts, axis=1), w_ref[...])


def _outproj(oa, ob, oc, od, gg, w, x2):
    m = x2.shape[0]
    mix_spec = pl.BlockSpec((TM_OUT, GROUP_WIDTH), lambda i: (i, 0))
    return pl.pallas_call(
        _outproj_kernel,
        grid=(m // TM_OUT,),
        in_specs=[mix_spec, mix_spec, mix_spec, mix_spec,
                  pl.BlockSpec((4, GROUP_WIDTH), lambda i: (0, 0)),
                  pl.BlockSpec((D_MODEL, D_MODEL), lambda i: (0, 0), pipeline_mode=pl.Buffered(1)),
                  pl.BlockSpec((TM_OUT, D_MODEL), lambda i: (i, 0))],
        out_specs=pl.BlockSpec((TM_OUT, D_MODEL), lambda i: (i, 0)),
        out_shape=jax.ShapeDtypeStruct((m, D_MODEL), F32),
        compiler_params=_params("parallel"),
        name="outproj",
    )(oa, ob, oc, od, gg, w, x2)


def _ffn_kernel(x_ref, g_ref, wg_ref, wu_ref, wd_ref, o_ref, h_ref):
    f = pl.program_id(1)

    @pl.when(f == 0)
    def _():
        x = x_ref[...]
        ms = jnp.mean(x * x, axis=-1, keepdims=True)
        h_ref[...] = ((x * lax.rsqrt(ms + 1e-6)) * g_ref[...]).astype(BF16)
        o_ref[...] = x

    h = h_ref[...]
    half = TF_FFN // 2
    pre = [(_dot(h, wg_ref[:, s:s + half]), _dot(h, wu_ref[:, s:s + half])) for s in (0, half)]
    down = None
    for s, (gate, up) in zip((0, half), pre):
        act = ((gate * jax.nn.sigmoid(gate)) * up).astype(BF16)
        d = _dot(act, wd_ref[s:s + half, :])
        down = d if down is None else down + d
    o_ref[...] += down


def _ffn(x2, gain, wg, wu, wd):
    m = x2.shape[0]
    return pl.pallas_call(
        _ffn_kernel,
        grid=(m // TM_FFN, D_FF // TF_FFN),
        in_specs=[pl.BlockSpec((TM_FFN, D_MODEL), lambda i, f: (i, 0)),
                  pl.BlockSpec((1, D_MODEL), lambda i, f: (0, 0)),
                  pl.BlockSpec((D_MODEL, TF_FFN), lambda i, f: (0, f)),
                  pl.BlockSpec((D_MODEL, TF_FFN), lambda i, f: (0, f)),
                  pl.BlockSpec((TF_FFN, D_MODEL), lambda i, f: (f, 0))],
        out_specs=pl.BlockSpec((TM_FFN, D_MODEL), lambda i, f: (i, 0)),
        out_shape=jax.ShapeDtypeStruct((m, D_MODEL), F32),
        scratch_shapes=[pltpu.VMEM((TM_FFN, D_MODEL), BF16)],
        compiler_params=_params("parallel", "arbitrary"),
        name="ffn",
    )(x2, gain, wg, wu, wd)


def _static_tables():
    i = np.arange(TQ)[:, None]
    j = np.arange(TK)[None, :]
    dist_t = np.arange(N_KT)[:, None, None] * TK + i[None] - j[None]
    bkt_t = np.where(dist_t >= 0, _rel_bucket_np(dist_t), MASKED_BUCKET)
    dist_w = dist_t[:WIN_TILES]
    bkt_w = np.where((dist_w >= 0) & (dist_w < WINDOW), _rel_bucket_np(dist_w), MASKED_BUCKET)
    t = np.arange(SEQ)[:, None]
    n = np.arange(LANES)[None, :]
    bkt_c = _rel_bucket_np(t - (n * CMP_STRIDE + CMP_LEN - 1))
    c0 = np.arange(LANES)[:, None] * CMP_STRIDE
    s0 = np.arange(LANES)[None, :] * SLC_LEN
    ov = np.minimum(c0 + CMP_LEN, s0 + SLC_LEN) - np.maximum(c0, s0)
    ov = np.maximum(ov, 0) / CMP_LEN
    ov[N_CMP:, :] = 0.0
    ov[:, N_SLC:] = 0.0
    key_blk = (np.arange(N_KT)[:, None] * TK + np.arange(TK)[None, :]) // SLC_LEN
    expand = (np.arange(LANES)[None, :, None] == key_blk[:, None, :]).astype(np.float32)
    tri = (np.arange(TK)[:, None] > np.arange(TK)[None, :]).astype(np.float32)
    tri = np.concatenate([tri, np.ones((TK, TK), np.float32)], axis=1)
    gate_expand = np.zeros((3, LANES, GROUP_WIDTH), np.float32)
    for i in range(3):
        for h in range(N_HEADS):
            gate_expand[i, 3 * h + i, h * HEAD_DIM:(h + 1) * HEAD_DIM] = 1.0
    return (bkt_t.reshape(-1, TK).astype(np.int32), bkt_w.reshape(-1, TK).astype(np.int32), bkt_c,
            ov.astype(np.float32), expand, tri, gate_expand)


def _w_in_layout_kernel(w_ref, o_ref):
    w = w_ref[0]
    n_gate = 3 * N_HEADS
    lane = lax.broadcasted_iota(jnp.int32, (w.shape[0], 2 * LANES), 1)
    tail = jnp.where(lane < n_gate, w[:, 1280:1280 + 2 * LANES], 0.0)
    o_ref[0] = jnp.concatenate([w[:, 0:512], w[:, 1304:5400], w[:, 512:1280], tail], axis=1).astype(BF16)


def _permute_w_in(w_in):
    depth = w_in.shape[0]
    tr = 256
    return pl.pallas_call(
        _w_in_layout_kernel,
        grid=(depth, D_MODEL // tr),
        in_specs=[pl.BlockSpec((1, tr, W_IN_COLS), lambda l, i: (l, i, 0))],
        out_specs=pl.BlockSpec((1, tr, PROJ_COLS), lambda l, i: (l, i, 0)),
        out_shape=jax.ShapeDtypeStruct((depth, D_MODEL, PROJ_COLS), BF16),
        compiler_params=_params("parallel", "parallel"),
        name="w_in_layout",
    )(w_in)


def kernel(x, w_in, w_out, norm_mix, norm_ffn, q_gain, k_gain, cmp_pos, cmp_w1, cmp_w2, rel_table, conv_w,
           sgu_w, sgu_b, group_gain, w_ffn_gate, w_ffn_up, w_ffn_down):
    batch, seq, d_model = x.shape
    assert seq == SEQ and d_model == D_MODEL
    depth = w_in.shape[0]
    m = batch * seq
    nhb = SEQ // CMP_STRIDE

    bkt_t, bkt_w, bkt_c, ov_np, expand_np, tri_np, gx_np = _static_tables()

    def tiles(b):
        return b.reshape(N_HEADS, -1, TQ, TK).transpose(1, 0, 2, 3)

    bias_t = tiles(_bias_expand(rel_table, bkt_t))
    bias_w = tiles(_bias_expand(rel_table, bkt_w))
    bias_c = _bias_expand(rel_table, bkt_c)
    overlap = jnp.asarray(_stack_rows(ov_np, 3), BF16)
    expand = jnp.asarray(expand_np, BF16)
    tri = jnp.asarray(_stack_rows(tri_np, 2), BF16)
    gate_expand = jnp.asarray(np.concatenate([gx_np] * 3, axis=1), BF16)

    w_in_p = _permute_w_in(w_in)
    w_out_b = w_out.astype(BF16)
    wg_b = w_ffn_gate.astype(BF16)
    wu_b = w_ffn_up.astype(BF16)
    wd_b = w_ffn_down.astype(BF16)
    w1_b = cmp_w1.astype(BF16)
    w2_b = cmp_w2.astype(BF16)
    pos = cmp_pos.reshape(depth, 2, 1, CMP_LEN * HEAD_DIM)
    kgain2 = jnp.tile(k_gain, (1, 2))
    sgu_bx = jnp.repeat(jnp.swapaxes(sgu_b, 1, 2), HEAD_DIM, axis=2)

    x2 = x.reshape(m, D_MODEL)
    for l in range(depth):
        proj = _inproj(x2, norm_mix[l][None], w_in_p[l])
        w1_blk, w2_blk = _compress_weights(cmp_w1[l], cmp_w2[l])
        kc, vc, kst, vsx, kwt, vwx, dk_t, dv = _nsa_prep(proj, w1_blk, w2_blk, _posterm(pos[l], w1_b[l]),
                                                         kgain2[l][None], batch)
        o_a = _nsa_attention(proj, kc, vc, kst, vsx, kwt, vwx,
                             bias_c, bias_t, bias_w, overlap, expand, gate_expand, q_gain[l][None], batch)
        o_d = _stickbreak(proj, dk_t, dv, tri, batch)
        o_b, o_c = _conv_sgu(proj, conv_w[l], sgu_w[l], sgu_bx[l])
        x2 = _outproj(o_a, o_b, o_c, o_d, group_gain[l].reshape(4, GROUP_WIDTH), w_out_b[l], x2)
        x2 = _ffn(x2, norm_ffn[l][None], wg_b[l], wu_b[l], wd_b[l])
    return x2.reshape(batch, seq, d_model)
```

```python
import math

import numpy as np
import jax
import jax.numpy as jnp
from jax import lax
from jax.experimental import pallas as pl
from jax.experimental.pallas import tpu as pltpu

F32 = jnp.float32
BF16 = jnp.bfloat16

D_MODEL = 2048
SEQ = 2048
HEAD_DIM = 64
GROUP_WIDTH = 512
N_HEADS = 8
NSA_KV_HEADS = 2
NSA_GQA = 4
CMP_LEN = 32
CMP_STRIDE = 16
SLC_LEN = 64
SLC_TOP = 16
WINDOW = 512
FORCE_BONUS = 1.0e3
N_BUCKETS = 32
MAX_DISTANCE = 1024
CONV_W = 3
CHUNK = 128
D_FF = 5632
NEG_INF = -1.0e30
N_CMP = (SEQ - CMP_LEN) // CMP_STRIDE + 1
N_SLC = SEQ // SLC_LEN
W_IN_COLS = 5400

LANES = 128
VMEM_LIMIT = 56 * 1024 * 1024

COL_Q = 0
COL_BG, COL_BC, COL_BH = 512, 1024, 1536
COL_CU, COL_CV = 2048, 2560
COL_DQ, COL_DK, COL_DV = 3072, 3584, 4096
COL_KV = 4608
COL_GATE = 5376
PROJ_COLS = 5632

TM_IN, TN_IN = 1024, 1408
TQ = 128
TK = 128
N_KT = SEQ // TK
WIN_TILES = WINDOW // TK + 1
TQ_SB = 256
TM_BC = 512
TM_OUT = 512
TM_FFN, TF_FFN = 1024, 512
MASKED_BUCKET = N_BUCKETS
LOG2E = math.log2(math.e)


def _params(*sem):
    return pltpu.CompilerParams(dimension_semantics=sem, vmem_limit_bytes=VMEM_LIMIT)


def _dot(a, b):
    return jnp.dot(a, b, preferred_element_type=F32)


def _dot_nt(a, b):
    return lax.dot_general(a, b, (((1,), (1,)), ((), ())), preferred_element_type=F32)


def _split_dot(a, b_stacked, terms):
    parts = []
    rem = a
    for i in range(terms):
        part = rem.astype(BF16)
        parts.append(part)
        if i + 1 < terms:
            rem = rem - part.astype(F32)
    return _dot(jnp.concatenate(parts, axis=1), b_stacked)


def _stack_rows(b_np, terms):
    return np.concatenate([b_np] * terms, axis=0)


def _gelu(x):
    c = math.sqrt(2.0 / math.pi)
    return x * (0.5 * (1.0 + jnp.tanh(c * (x + 0.044715 * (x * x * x)))))


def _rel_bucket_np(dist):
    n = np.maximum(dist, 0)
    max_exact = N_BUCKETS // 2
    nf = np.maximum(n, 1).astype(np.float64)
    large = max_exact + (np.log(nf / max_exact) / math.log(MAX_DISTANCE / max_exact)
                         * (N_BUCKETS - max_exact)).astype(np.int32)
    large = np.minimum(large, N_BUCKETS - 1)
    return np.where(n < max_exact, n, large).astype(np.int32)


def _bias_expand_kernel(tab_ref, bkt_ref, o_ref):
    bkt = bkt_ref[...]
    for h in range(N_HEADS):
        acc = jnp.full(bkt.shape, NEG_INF, F32)
        for b in range(N_BUCKETS):
            acc = jnp.where(bkt == b, tab_ref[b, h] * LOG2E, acc)
        o_ref[h] = acc


def _bias_expand(rel_table, bucket_np):
    rows = bucket_np.shape[0]
    tr = 128
    return pl.pallas_call(
        _bias_expand_kernel,
        grid=(rows // tr,),
        in_specs=[pl.BlockSpec(memory_space=pltpu.SMEM),
                  pl.BlockSpec((tr, LANES), lambda i: (i, 0))],
        out_specs=pl.BlockSpec((N_HEADS, tr, LANES), lambda i: (0, i, 0)),
        out_shape=jax.ShapeDtypeStruct((N_HEADS, rows, LANES), F32),
        compiler_params=_params("parallel"),
        name="bias_expand",
    )(rel_table, jnp.asarray(bucket_np))


def _inproj_kernel(x_ref, g_ref, w_ref, o_ref, h_ref):
    @pl.when(pl.program_id(1) == 0)
    def _():
        x = x_ref[...]
        ms = jnp.mean(x * x, axis=-1, keepdims=True)
        h_ref[...] = ((x * lax.rsqrt(ms + 1e-6)) * g_ref[...]).astype(BF16)

    o_ref[...] = _dot(h_ref[...], w_ref[0])


def _inproj(x2, gain, w, layer):
    m = x2.shape[0]
    return pl.pallas_call(
        _inproj_kernel,
        grid=(m // TM_IN, PROJ_COLS // TN_IN),
        in_specs=[pl.BlockSpec((TM_IN, D_MODEL), lambda i, j: (i, 0)),
                  pl.BlockSpec((1, D_MODEL), lambda i, j: (0, 0)),
                  pl.BlockSpec((1, D_MODEL, TN_IN), lambda i, j: (layer, 0, j))],
        out_specs=pl.BlockSpec((TM_IN, TN_IN), lambda i, j: (i, j)),
        out_shape=jax.ShapeDtypeStruct((m, PROJ_COLS), F32),
        scratch_shapes=[pltpu.VMEM((TM_IN, D_MODEL), BF16)],
        compiler_params=_params("parallel", "arbitrary"),
        name="inproj",
    )(x2, gain, w)


def _seg_rms(x, gain2):
    lane = lax.broadcasted_iota(jnp.int32, x.shape, 1)
    x2 = x * x
    lo = lane < HEAD_DIM
    s0 = jnp.sum(jnp.where(lo, x2, 0.0), axis=-1, keepdims=True)
    s1 = jnp.sum(jnp.where(lo, 0.0, x2), axis=-1, keepdims=True)
    ms = jnp.where(lo, s0, s1) * (1.0 / HEAD_DIM)
    return (x * lax.rsqrt(ms + 1e-6)) * gain2


def _nsa_prep_kernel(kv_ref, kcin_ref, vcin_ref, dk_ref, dv_ref, w1_ref, w2_ref, posterm_ref, kg2_ref,
                     kc_ref, vc_ref, kst_ref, vsx_ref, kwt_ref, vwx_ref, dkt_ref, dvb_ref):
    kg2 = kg2_ref[...]
    lane = lax.broadcasted_iota(jnp.int32, (SEQ, LANES), 1)
    for col, kt_ref, vx_ref in ((2, kst_ref, vsx_ref), (4, kwt_ref, vwx_ref)):
        kn = _seg_rms(kv_ref[:, col * LANES:(col + 1) * LANES], kg2)
        for kj in range(N_KT):
            tile_t = kn[kj * TK:(kj + 1) * TK].T
            for g in range(NSA_KV_HEADS):
                kt_ref[0, g, kj] = tile_t[g * HEAD_DIM:(g + 1) * HEAD_DIM].astype(BF16)
        v = kv_ref[:, (col + 1) * LANES:(col + 2) * LANES]
        vx_ref[0, 0] = jnp.where(lane < HEAD_DIM, v, 1.0).astype(BF16)
        vx_ref[0, 1] = jnp.where(lane < HEAD_DIM, pltpu.roll(v, HEAD_DIM, axis=1), 1.0).astype(BF16)

    for kj in range(N_KT):
        dkt_ref[0, kj] = dk_ref[kj * TK:(kj + 1) * TK, :].T.astype(BF16)
    dvb_ref[...] = dv_ref[...].astype(BF16)

    nhb = SEQ // CMP_STRIDE
    first = None
    second = None
    for l in range(CMP_STRIDE):
        rows_l = pl.ds(l, nhb, stride=CMP_STRIDE)
        x_l = jnp.concatenate([kcin_ref[rows_l, :], vcin_ref[rows_l, :]], axis=1).astype(BF16)
        a = _dot(x_l, w1_ref[0, l])
        b = _dot(x_l, w1_ref[1, l])
        first = a if first is None else first + a
        second = b if second is None else second + b
    pre = first + pltpu.roll(second, nhb - 1, axis=0) + posterm_ref[...]
    out = _dot(_gelu(pre).astype(BF16), w2_ref[...])
    keys = _seg_rms(out[:, :LANES], kg2).astype(BF16)
    vals = out[:, LANES:].astype(BF16)
    for g in range(NSA_KV_HEADS):
        kc_ref[0, g] = keys[:, g * HEAD_DIM:(g + 1) * HEAD_DIM]
        vc_ref[0, g] = vals[:, g * HEAD_DIM:(g + 1) * HEAD_DIM]


def _compress_weights(w1, w2):
    n_stream = 2 * NSA_KV_HEADS
    eye = jnp.eye(n_stream, dtype=w1.dtype)
    w1r = w1.reshape(2, 2, CMP_STRIDE, HEAD_DIM, HEAD_DIM).transpose(1, 2, 0, 3, 4)
    w1s = jnp.repeat(w1r, NSA_KV_HEADS, axis=2)
    w1_blk = jnp.einsum('hlcde,cf->hlcdfe', w1s, eye).reshape(2, CMP_STRIDE, n_stream * HEAD_DIM, n_stream * HEAD_DIM)
    w2s = jnp.repeat(w2, NSA_KV_HEADS, axis=0)
    w2_blk = jnp.einsum('cde,cf->cdfe', w2s, eye).reshape(n_stream * HEAD_DIM, n_stream * HEAD_DIM)
    return w1_blk.astype(BF16), w2_blk.astype(BF16)


def _posterm_kernel(pos_ref, w1_ref, o_ref):
    for i in range(2):
        pos = jnp.broadcast_to(pos_ref[i], (8, CMP_LEN * HEAD_DIM)).astype(BF16)
        term = _dot(pos, w1_ref[i])
        o_ref[:, i * LANES:(i + 1) * LANES] = jnp.concatenate([term] * NSA_KV_HEADS, axis=1)


def _posterm(pos, w1):
    return pl.pallas_call(
        _posterm_kernel,
        out_shape=jax.ShapeDtypeStruct((8, 2 * NSA_KV_HEADS * HEAD_DIM), F32),
        name="cmp_posterm",
    )(pos, w1)[0:1]


def _nsa_prep(proj, w1_blk, w2_blk, posterm, kgain2, batch):
    nhb = SEQ // CMP_STRIDE
    cmp_shape = jax.ShapeDtypeStruct((batch, NSA_KV_HEADS, nhb, HEAD_DIM), BF16)
    cmp_spec = pl.BlockSpec((1, NSA_KV_HEADS, nhb, HEAD_DIM), lambda b: (b, 0, 0, 0))
    kt_shape = jax.ShapeDtypeStruct((batch, NSA_KV_HEADS, N_KT, HEAD_DIM, TK), BF16)
    kt_spec = pl.BlockSpec((1, NSA_KV_HEADS, N_KT, HEAD_DIM, TK), lambda b: (b, 0, 0, 0, 0))
    vx_shape = jax.ShapeDtypeStruct((batch, NSA_KV_HEADS, SEQ, LANES), BF16)
    vx_spec = pl.BlockSpec((1, NSA_KV_HEADS, SEQ, LANES), lambda b: (b, 0, 0, 0))
    dkt_shape = jax.ShapeDtypeStruct((batch, N_KT, GROUP_WIDTH, TK), BF16)
    dkt_spec = pl.BlockSpec((1, N_KT, GROUP_WIDTH, TK), lambda b: (b, 0, 0, 0))
    dvb_shape = jax.ShapeDtypeStruct((batch * SEQ, GROUP_WIDTH), BF16)
    dvb_spec = pl.BlockSpec((SEQ, GROUP_WIDTH), lambda b: (b, 0))
    return pl.pallas_call(
        _nsa_prep_kernel,
        grid=(batch,),
        in_specs=[pl.BlockSpec((SEQ, 6 * LANES), lambda b: (b, COL_KV // (6 * LANES))),
                  pl.BlockSpec((SEQ, LANES), lambda b: (b, COL_KV // LANES)),
                  pl.BlockSpec((SEQ, LANES), lambda b: (b, COL_KV // LANES + 1)),
                  pl.BlockSpec((SEQ, GROUP_WIDTH), lambda b: (b, COL_DK // GROUP_WIDTH)),
                  pl.BlockSpec((SEQ, GROUP_WIDTH), lambda b: (b, COL_DV // GROUP_WIDTH)),
                  pl.BlockSpec((2, CMP_STRIDE, 2 * LANES, 2 * LANES), lambda b: (0, 0, 0, 0)),
                  pl.BlockSpec((2 * LANES, 2 * LANES), lambda b: (0, 0)),
                  pl.BlockSpec((1, 2 * LANES), lambda b: (0, 0)),
                  pl.BlockSpec((1, LANES), lambda b: (0, 0))],
        out_specs=[cmp_spec, cmp_spec, kt_spec, vx_spec, kt_spec, vx_spec, dkt_spec, dvb_spec],
        out_shape=[cmp_shape, cmp_shape, kt_shape, vx_shape, kt_shape, vx_shape, dkt_shape, dvb_shape],
        compiler_params=_params("parallel"),
        name="nsa_prep",
    )(proj, proj, proj, proj, proj, w1_blk, w2_blk, posterm, kgain2)


def _nsa_kernel(q_ref, gate_ref, kc_ref, vc_ref, kst_ref, vsx_ref, kwt_ref, vwx_ref,
                bc_ref, bt_ref, bw_ref, ov_ref, ex_ref, gx_ref, qg_ref, o_ref, *scratch):
    n_chain = 2 * NSA_KV_HEADS
    m_refs, acc_refs = scratch[:n_chain], scratch[n_chain:]
    qi = pl.program_id(1)
    rows = NSA_GQA * TQ
    q = q_ref[...]
    qg = qg_ref[...] * (HEAD_DIM ** -0.5 * LOG2E)
    qn = []
    for h in range(N_HEADS):
        seg = q[:, h * HEAD_DIM:(h + 1) * HEAD_DIM]
        ms = jnp.mean(seg * seg, axis=-1, keepdims=True)
        qn.append(((seg * lax.rsqrt(ms + 1e-6)) * qg).astype(BF16))
    qst = [jnp.concatenate(qn[g * NSA_GQA:(g + 1) * NSA_GQA], axis=0) for g in range(NSA_KV_HEADS)]
    gates = jax.nn.sigmoid(gate_ref[...])

    t1 = qi * TQ + lax.broadcasted_iota(jnp.int32, (TQ, LANES), 0)
    t4 = jnp.concatenate([t1] * NSA_GQA, axis=0)
    lane1 = lax.broadcasted_iota(jnp.int32, (TQ, LANES), 1)
    lane4 = lax.broadcasted_iota(jnp.int32, (rows, LANES), 1)
    blk_t = lax.broadcasted_iota(jnp.int32, (N_SLC, TQ), 0)

    kv_groups = range(NSA_KV_HEADS)
    mask_c = (t4 - (lane4 * CMP_STRIDE + (CMP_LEN - 1)) >= 0) & (lane4 < N_CMP)
    s_c = [_dot_nt(qst[g], kc_ref[0, g]) + bc_ref[g * NSA_GQA:(g + 1) * NSA_GQA].reshape(rows, LANES)
           for g in kv_groups]
    p_c = []
    for g in kv_groups:
        s = jnp.where(mask_c, s_c[g], NEG_INF)
        m = jnp.max(s, axis=-1, keepdims=True)
        e = jnp.where(mask_c, jnp.exp2(s - m), 0.0)
        l = jnp.sum(e, axis=-1, keepdims=True)
        p_c.append(e * jnp.where(l > 0.0, 1.0 / l, 0.0))
    o_cmp = [_dot(p_c[g].astype(BF16), vc_ref[0, g]) for g in kv_groups]
    imps = []
    for g in kv_groups:
        p = p_c[g]
        psum = p[0:TQ] + p[TQ:2 * TQ] + p[2 * TQ:3 * TQ] + p[3 * TQ:4 * TQ]
        imps.append(_split_dot(psum, ov_ref[...], 3))

    cur = t1 >> 6
    valid = (lane1 * SLC_LEN <= t1) & (lane1 < N_SLC)
    forced = (lane1 == 0) | (lane1 == cur) | (lane1 == cur - 1)
    sel = []
    for g in kv_groups:
        score = jnp.where(valid, imps[g] + jnp.where(forced, FORCE_BONUS, 0.0), NEG_INF)
        st = score.T[0:N_SLC]
        rank = jnp.zeros((N_SLC, TQ), F32)
        for k in range(N_SLC):
            ck = st[k:k + 1, :]
            before = (ck > st) | ((ck == st) & (blk_t > k))
            rank = rank + jnp.where(before, 1.0, 0.0)
        sel_t = jnp.where(rank < float(SLC_TOP), 1.0, 0.0)
        sel_t = jnp.concatenate([sel_t, jnp.zeros((LANES - N_SLC, TQ), F32)], axis=0)
        sel.append(sel_t.T.astype(BF16))

    for c in range(n_chain):
        m_refs[c][...] = jnp.full((rows, LANES), NEG_INF, F32)
        acc_refs[c][...] = jnp.zeros((rows, LANES), F32)

    def flash_step(c, qk, bias, vx):
        sc = qk + bias
        m_old = m_refs[c][...]
        m_new = jnp.maximum(m_old, jnp.max(sc, axis=-1, keepdims=True))
        alpha = jnp.exp2(m_old - m_new)
        pe = jnp.exp2(sc - jnp.concatenate([m_new] * (sc.shape[1] // LANES), axis=1))
        acc_refs[c][...] = alpha * acc_refs[c][...] + _dot(pe.astype(BF16), vx)
        m_refs[c][...] = m_new

    def slc_logits(kj, g, n=1):
        tiles = range(n)
        expand = jnp.concatenate([ex_ref[kj + j] for j in tiles], axis=1)
        unsel = (_dot(sel[g], expand) - 1.0) * (-NEG_INF)
        return _dot(qst[g], jnp.concatenate([kst_ref[0, g, kj + j] for j in tiles], axis=1)), unsel

    def slc_update(kj, g, logits, n=1):
        qk, unsel = logits
        off = pl.multiple_of(kj * TK, TK)
        heads = slice(g * NSA_GQA, (g + 1) * NSA_GQA)
        bias = jnp.concatenate([bt_ref[qi - kj - j, heads] for j in range(n)], axis=2)
        bias = (bias + unsel[None]).reshape(rows, n * TK)
        flash_step(g, qk, bias, vsx_ref[0, g, pl.ds(off, n * TK), :])

    def win_update(kj, g, qk):
        off = pl.multiple_of(kj * TK, TK)
        bias = bw_ref[qi - kj, g * NSA_GQA:(g + 1) * NSA_GQA].reshape(rows, TK)
        flash_step(NSA_KV_HEADS + g, qk, bias, vwx_ref[0, g, pl.ds(off, TK), :])

    lo = jnp.maximum(qi - (WIN_TILES - 1), 0)
    groups = range(NSA_KV_HEADS)

    def far_pair_body(i, carry):
        kj = 2 * i
        logits = [slc_logits(kj, g, 2) for g in groups]
        for g in groups:
            slc_update(kj, g, logits[g], 2)
        return carry

    def far_body(kj, carry):
        logits = [slc_logits(kj, g) for g in groups]
        for g in groups:
            slc_update(kj, g, logits[g])
        return carry

    n_pair = lo // 2
    lax.fori_loop(0, n_pair, far_pair_body, 0)
    lax.fori_loop(2 * n_pair, lo, far_body, 0)

    def near_body(kj, carry):
        s_logits = [slc_logits(kj, g) for g in groups]
        w_logits = [_dot(qst[g], kwt_ref[0, g, kj]) for g in groups]
        for g in groups:
            slc_update(kj, g, s_logits[g])
        for g in groups:
            win_update(kj, g, w_logits[g])
        return carry

    lax.fori_loop(lo, qi + 1, near_body, 0)

    def head_major(stacked):
        return jnp.concatenate([stacked[g][r * TQ:(r + 1) * TQ, :HEAD_DIM]
                                for g in groups for r in range(NSA_GQA)], axis=-1)

    def normalised(acc_ref):
        acc = acc_ref[...]
        return acc * (1.0 / pltpu.roll(acc, HEAD_DIM, axis=1))

    branches = (o_cmp,
                [normalised(acc_refs[g]) for g in groups],
                [normalised(acc_refs[NSA_KV_HEADS + g]) for g in groups])
    out = None
    for i, branch in enumerate(branches):
        term = _split_dot(gates, gx_ref[i], 3) * head_major(branch)
        out = term if out is None else out + term
    o_ref[...] = out


def _nsa_attention(proj, kc, vc, kst, vsx, kwt, vwx, bias_c, bias_t, bias_w, overlap, expand, gate_expand,
                   qgain, batch):
    nq = SEQ // TQ
    nhb = SEQ // CMP_STRIDE
    cmp_spec = pl.BlockSpec((1, NSA_KV_HEADS, nhb, HEAD_DIM), lambda b, i: (b, 0, 0, 0))
    kt_spec = pl.BlockSpec((1, NSA_KV_HEADS, N_KT, HEAD_DIM, TK), lambda b, i: (b, 0, 0, 0, 0))
    vx_spec = pl.BlockSpec((1, NSA_KV_HEADS, SEQ, LANES), lambda b, i: (b, 0, 0, 0))
    return pl.pallas_call(
        _nsa_kernel,
        grid=(batch, nq),
        in_specs=[pl.BlockSpec((TQ, GROUP_WIDTH), lambda b, i: (b * nq + i, COL_Q // GROUP_WIDTH)),
                  pl.BlockSpec((TQ, LANES), lambda b, i: (b * nq + i, COL_GATE // LANES)),
                  cmp_spec, cmp_spec, kt_spec, vx_spec, kt_spec, vx_spec,
                  pl.BlockSpec((N_HEADS, TQ, LANES), lambda b, i: (0, i, 0)),
                  pl.BlockSpec((N_KT, N_HEADS, TQ, TK), lambda b, i: (0, 0, 0, 0)),
                  pl.BlockSpec((WIN_TILES, N_HEADS, TQ, TK), lambda b, i: (0, 0, 0, 0)),
                  pl.BlockSpec((3 * LANES, LANES), lambda b, i: (0, 0)),
                  pl.BlockSpec((N_KT, LANES, TK), lambda b, i: (0, 0, 0)),
                  pl.BlockSpec((3, 3 * LANES, GROUP_WIDTH), lambda b, i: (0, 0, 0)),
                  pl.BlockSpec((1, HEAD_DIM), lambda b, i: (0, 0))],
        out_specs=pl.BlockSpec((TQ, GROUP_WIDTH), lambda b, i: (b * nq + i, 0)),
        out_shape=jax.ShapeDtypeStruct((batch * SEQ, GROUP_WIDTH), F32),
        scratch_shapes=[pltpu.VMEM((NSA_GQA * TQ, LANES), F32)] * (4 * NSA_KV_HEADS),
        compiler_params=_params("parallel", "arbitrary"),
        name="nsa_attention",
    )(proj, proj, kc, vc, kst, vsx, kwt, vwx, bias_c, bias_t, bias_w, overlap, expand, gate_expand, qgain)


def _stickbreak_kernel(q_ref, kt_ref, v_ref, tri_ref, o_ref, *scratch):
    acc_ref, tc_ref = scratch[:N_HEADS], scratch[N_HEADS:]
    qi = pl.program_id(1)
    q = q_ref[...] * (HEAD_DIM ** -0.5)
    qh = [q[:, h * HEAD_DIM:(h + 1) * HEAD_DIM].astype(BF16) for h in range(N_HEADS)]
    t = qi * TQ_SB + lax.broadcasted_iota(jnp.int32, (TQ_SB, TK), 0)
    lane = lax.broadcasted_iota(jnp.int32, (TQ_SB, TK), 1)
    tri = tri_ref[...]
    for h in range(N_HEADS):
        acc_ref[h][...] = jnp.zeros((TQ_SB, LANES), F32)
        tc_ref[h][...] = jnp.zeros((TQ_SB, TK), F32)

    def tile_step(kj, masked, row0=0):
        off = pl.multiple_of(kj * TK, TK)
        heads = range(N_HEADS)
        rs = slice(row0, TQ_SB)
        if masked:
            ok = ((lane + off) < t)[rs]
        z = [_dot(qh[h][rs], kt_ref[0, kj, h * HEAD_DIM:(h + 1) * HEAD_DIM, :]) for h in heads]
        log_beta, r = [], []
        for h in heads:
            soft = jnp.log(1.0 + jnp.exp2(jnp.abs(z[h]) * (-LOG2E)))
            lb = jnp.minimum(z[h], 0.0) - soft
            log_1m = lb - z[h]
            if masked:
                log_1m = jnp.where(ok, log_1m, 0.0)
            log_beta.append(lb)
            r.append(_split_dot(log_1m, tri, 2))
        for h in heads:
            vt = v_ref[pl.ds(off, TK), (h // 2) * LANES:(h // 2 + 1) * LANES]
            a = jnp.exp(log_beta[h] + r[h][:, :TK] + tc_ref[h][rs, :])
            if masked:
                a = jnp.where(ok, a, 0.0)
            acc_ref[h][rs, :] += _dot(a.astype(BF16), vt)
            tc_ref[h][rs, :] += r[h][:, TK:]

    n_diag = TQ_SB // TK
    for d in range(n_diag):
        tile_step((qi + 1) * n_diag - 1 - d, True, (n_diag - 1 - d) * TK)

    def body(i, carry):
        tile_step(qi * n_diag - 1 - i, False)
        return carry

    lax.fori_loop(0, qi * n_diag, body, 0)

    lane_o = lax.broadcasted_iota(jnp.int32, (TQ_SB, LANES), 1)
    for pair in range(N_HEADS // 2):
        o_ref[:, pair * LANES:(pair + 1) * LANES] = jnp.where(
            lane_o < HEAD_DIM, acc_ref[2 * pair][...], acc_ref[2 * pair + 1][...])


def _stickbreak(proj, kt, v, tri, batch):
    nq = SEQ // TQ_SB
    return pl.pallas_call(
        _stickbreak_kernel,
        grid=(batch, nq),
        in_specs=[pl.BlockSpec((TQ_SB, GROUP_WIDTH), lambda b, i: (b * nq + i, COL_DQ // GROUP_WIDTH)),
                  pl.BlockSpec((1, N_KT, GROUP_WIDTH, TK), lambda b, i: (b, 0, 0, 0)),
                  pl.BlockSpec((SEQ, GROUP_WIDTH), lambda b, i: (b, 0)),
                  pl.BlockSpec((2 * TK, 2 * TK), lambda b, i: (0, 0))],
        out_specs=pl.BlockSpec((TQ_SB, GROUP_WIDTH), lambda b, i: (b * nq + i, 0)),
        out_shape=jax.ShapeDtypeStruct((batch * SEQ, GROUP_WIDTH), F32),
        scratch_shapes=[pltpu.VMEM((TQ_SB, LANES), F32)] * (2 * N_HEADS),
        compiler_params=_params("parallel", "arbitrary"),
        name="stickbreak",
    )(proj, kt, v, tri)


def _conv_sgu_kernel(bg_ref, cg_ref, hh_ref, cgp_ref, hhp_ref, cu_ref, cv_ref, cw_ref, sw_ref, sb_ref,
                     ob_ref, oc_ref):
    i = pl.program_id(0)
    z = cg_ref[...] * hh_ref[...]
    first = (i % (SEQ // TM_BC)) == 0
    zp = jnp.where(first, 0.0, cgp_ref[...] * hhp_ref[...])
    row = lax.broadcasted_iota(jnp.int32, z.shape, 0)
    p1 = jnp.broadcast_to(zp[7:8], z.shape)
    p2 = jnp.broadcast_to(zp[6:7], z.shape)
    z1 = jnp.where(row == 0, p1, pltpu.roll(z, 1, axis=0))
    z2 = jnp.where(row == 0, p2, jnp.where(row == 1, p1, pltpu.roll(z, 2, axis=0)))
    cw = cw_ref[...]
    y = cw[0:1] * z2
    y = y + cw[1:2] * z1
    y = y + cw[2:3] * z
    ob_ref[...] = bg_ref[...] * y

    u = _gelu(cu_ref[...])
    v = _gelu(cv_ref[...])
    mu = jnp.mean(v, axis=-1, keepdims=True)
    vc = v - mu
    var = jnp.mean(vc * vc, axis=-1, keepdims=True)
    vn = (vc * lax.rsqrt(var + 1e-5)).astype(BF16)
    pr = lax.broadcasted_iota(jnp.int32, (CHUNK, CHUNK), 0)
    pc = lax.broadcasted_iota(jnp.int32, (CHUNK, CHUNK), 1)
    ws = [jnp.where(pr >= pc, sw_ref[h], 0.0).astype(BF16) for h in range(N_HEADS)]
    for c in range(TM_BC // CHUNK):
        rs = slice(c * CHUNK, (c + 1) * CHUNK)
        s = jnp.concatenate(
            [_dot(ws[h], vn[rs, h * HEAD_DIM:(h + 1) * HEAD_DIM]) for h in range(N_HEADS)], axis=-1)
        oc_ref[rs, :] = u[rs] * (s + sb_ref[...])


def _conv_sgu(proj, conv_w, sgu_w, sgu_bx):
    m = proj.shape[0]
    gw = GROUP_WIDTH

    def col(c):
        return pl.BlockSpec((TM_BC, gw), lambda i: (i, c // gw))

    def prev(c):
        return pl.BlockSpec((8, gw), lambda i: (jnp.maximum(i * (TM_BC // 8) - 1, 0), c // gw))

    out = jax.ShapeDtypeStruct((m, gw), F32)
    return pl.pallas_call(
        _conv_sgu_kernel,
        grid=(m // TM_BC,),
        in_specs=[col(COL_BG), col(COL_BC), col(COL_BH), prev(COL_BC), prev(COL_BH), col(COL_CU), col(COL_CV),
                  pl.BlockSpec((CONV_W, gw), lambda i: (0, 0)),
                  pl.BlockSpec((N_HEADS, CHUNK, CHUNK), lambda i: (0, 0, 0)),
                  pl.BlockSpec((CHUNK, gw), lambda i: (0, 0))],
        out_specs=[pl.BlockSpec((TM_BC, gw), lambda i: (i, 0))] * 2,
        out_shape=[out, out],
        compiler_params=_params("parallel"),
        name="conv_sgu",
    )(proj, proj, proj, proj, proj, proj, proj, conv_w, sgu_w, sgu_bx)


def _outproj_kernel(oa_ref, ob_ref, oc_ref, od_ref, gg_ref, w_ref, x_ref, o_ref):
    parts = []
    for i, ref in enumerate((oa_ref, ob_ref, oc_ref, od_ref)):
        o = ref[...]
        ms = jnp.mean(o * o, axis=-1, keepdims=True)
        parts.append(((o * lax.rsqrt(ms + 1e-6)) * gg_ref[i:i + 1]).astype(BF16))
    o_ref[...] = x_ref[...] + _dot(jnp.concatenate(parts, axis=1), w_ref[0])


def _outproj(oa, ob, oc, od, gg, w, layer, x2):
    m = x2.shape[0]
    mix_spec = pl.BlockSpec((TM_OUT, GROUP_WIDTH), lambda i: (i, 0))
    return pl.pallas_call(
        _outproj_kernel,
        grid=(m // TM_OUT,),
        in_specs=[mix_spec, mix_spec, mix_spec, mix_spec,
                  pl.BlockSpec((4, GROUP_WIDTH), lambda i: (0, 0)),
                  pl.BlockSpec((1, D_MODEL, D_MODEL), lambda i: (layer, 0, 0), pipeline_mode=pl.Buffered(1)),
                  pl.BlockSpec((TM_OUT, D_MODEL), lambda i: (i, 0))],
        out_specs=pl.BlockSpec((TM_OUT, D_MODEL), lambda i: (i, 0)),
        out_shape=jax.ShapeDtypeStruct((m, D_MODEL), F32),
        compiler_params=_params("parallel"),
        name="outproj",
    )(oa, ob, oc, od, gg, w, x2)


def _ffn_kernel(x_ref, g_ref, wg_ref, wu_ref, wd_ref, o_ref, h_ref):
    f = pl.program_id(1)

    @pl.when(f == 0)
    def _():
        x = x_ref[...]
        ms = jnp.mean(x * x, axis=-1, keepdims=True)
        h_ref[...] = ((x * lax.rsqrt(ms + 1e-6)) * g_ref[...]).astype(BF16)
        o_ref[...] = x

    h = h_ref[...]
    half = TF_FFN // 2
    pre = [(_dot(h, wg_ref[0, :, s:s + half]), _dot(h, wu_ref[0, :, s:s + half])) for s in (0, half)]
    down = None
    for s, (gate, up) in zip((0, half), pre):
        act = ((gate * jax.nn.sigmoid(gate)) * up).astype(BF16)
        d = _dot(act, wd_ref[0, s:s + half, :])
        down = d if down is None else down + d
    o_ref[...] += down


def _ffn(x2, gain, wg, wu, wd, layer):
    m = x2.shape[0]
    return pl.pallas_call(
        _ffn_kernel,
        grid=(m // TM_FFN, D_FF // TF_FFN),
        in_specs=[pl.BlockSpec((TM_FFN, D_MODEL), lambda i, f: (i, 0)),
                  pl.BlockSpec((1, D_MODEL), lambda i, f: (0, 0)),
                  pl.BlockSpec((1, D_MODEL, TF_FFN), lambda i, f: (layer, 0, f)),
                  pl.BlockSpec((1, D_MODEL, TF_FFN), lambda i, f: (layer, 0, f)),
                  pl.BlockSpec((1, TF_FFN, D_MODEL), lambda i, f: (layer, f, 0))],
        out_specs=pl.BlockSpec((TM_FFN, D_MODEL), lambda i, f: (i, 0)),
        out_shape=jax.ShapeDtypeStruct((m, D_MODEL), F32),
        scratch_shapes=[pltpu.VMEM((TM_FFN, D_MODEL), BF16)],
        compiler_params=_params("parallel", "arbitrary"),
        name="ffn",
    )(x2, gain, wg, wu, wd)


def _static_tables():
    i = np.arange(TQ)[:, None]
    j = np.arange(TK)[None, :]
    dist_t = np.arange(N_KT)[:, None, None] * TK + i[None] - j[None]
    bkt_t = np.where(dist_t >= 0, _rel_bucket_np(dist_t), MASKED_BUCKET)
    dist_w = dist_t[:WIN_TILES]
    bkt_w = np.where((dist_w >= 0) & (dist_w < WINDOW), _rel_bucket_np(dist_w), MASKED_BUCKET)
    t = np.arange(SEQ)[:, None]
    n = np.arange(LANES)[None, :]
    bkt_c = _rel_bucket_np(t - (n * CMP_STRIDE + CMP_LEN - 1))
    c0 = np.arange(LANES)[:, None] * CMP_STRIDE
    s0 = np.arange(LANES)[None, :] * SLC_LEN
    ov = np.minimum(c0 + CMP_LEN, s0 + SLC_LEN) - np.maximum(c0, s0)
    ov = np.maximum(ov, 0) / CMP_LEN
    ov[N_CMP:, :] = 0.0
    ov[:, N_SLC:] = 0.0
    key_blk = (np.arange(N_KT)[:, None] * TK + np.arange(TK)[None, :]) // SLC_LEN
    expand = (np.arange(LANES)[None, :, None] == key_blk[:, None, :]).astype(np.float32)
    tri = (np.arange(TK)[:, None] > np.arange(TK)[None, :]).astype(np.float32)
    tri = np.concatenate([tri, np.ones((TK, TK), np.float32)], axis=1)
    gate_expand = np.zeros((3, LANES, GROUP_WIDTH), np.float32)
    for i in range(3):
        for h in range(N_HEADS):
            gate_expand[i, 3 * h + i, h * HEAD_DIM:(h + 1) * HEAD_DIM] = 1.0
    return (bkt_t.reshape(-1, TK).astype(np.int32), bkt_w.reshape(-1, TK).astype(np.int32), bkt_c,
            ov.astype(np.float32), expand, tri, gate_expand)


def _w_in_layout_kernel(w_ref, o_ref):
    w = w_ref[0]
    n_gate = 3 * N_HEADS
    lane = lax.broadcasted_iota(jnp.int32, (w.shape[0], 2 * LANES), 1)
    tail = jnp.where(lane < n_gate, w[:, 1280:1280 + 2 * LANES], 0.0)
    o_ref[0] = jnp.concatenate([w[:, 0:512], w[:, 1304:5400], w[:, 512:1280], tail], axis=1).astype(BF16)


def _permute_w_in(w_in):
    depth = w_in.shape[0]
    tr = 256
    return pl.pallas_call(
        _w_in_layout_kernel,
        grid=(depth, D_MODEL // tr),
        in_specs=[pl.BlockSpec((1, tr, W_IN_COLS), lambda l, i: (l, i, 0))],
        out_specs=pl.BlockSpec((1, tr, PROJ_COLS), lambda l, i: (l, i, 0)),
        out_shape=jax.ShapeDtypeStruct((depth, D_MODEL, PROJ_COLS), BF16),
        compiler_params=_params("parallel", "parallel"),
        name="w_in_layout",
    )(w_in)


def kernel(x, w_in, w_out, norm_mix, norm_ffn, q_gain, k_gain, cmp_pos, cmp_w1, cmp_w2, rel_table, conv_w,
           sgu_w, sgu_b, group_gain, w_ffn_gate, w_ffn_up, w_ffn_down):
    batch, seq, d_model = x.shape
    assert seq == SEQ and d_model == D_MODEL
    depth = w_in.shape[0]
    m = batch * seq
    nhb = SEQ // CMP_STRIDE

    bkt_t, bkt_w, bkt_c, ov_np, expand_np, tri_np, gx_np = _static_tables()

    def tiles(b):
        return b.reshape(N_HEADS, -1, TQ, TK).transpose(1, 0, 2, 3)

    bias_t = tiles(_bias_expand(rel_table, bkt_t))
    bias_w = tiles(_bias_expand(rel_table, bkt_w))
    bias_c = _bias_expand(rel_table, bkt_c)
    overlap = jnp.asarray(_stack_rows(ov_np, 3), BF16)
    expand = jnp.asarray(expand_np, BF16)
    tri = jnp.asarray(_stack_rows(tri_np, 2), BF16)
    gate_expand = jnp.asarray(np.concatenate([gx_np] * 3, axis=1), BF16)

    w_in_p = _permute_w_in(w_in)
    w_out_b = w_out.astype(BF16)
    wg_b = w_ffn_gate.astype(BF16)
    wu_b = w_ffn_up.astype(BF16)
    wd_b = w_ffn_down.astype(BF16)
    w1_b = cmp_w1.astype(BF16)
    w2_b = cmp_w2.astype(BF16)
    pos = cmp_pos.reshape(depth, 2, 1, CMP_LEN * HEAD_DIM)
    kgain2 = jnp.tile(k_gain, (1, 2))
    sgu_bx = jnp.repeat(jnp.swapaxes(sgu_b, 1, 2), HEAD_DIM, axis=2)

    x2 = x.reshape(m, D_MODEL)
    for l in range(depth):
        proj = _inproj(x2, norm_mix[l][None], w_in_p, l)
        w1_blk, w2_blk = _compress_weights(cmp_w1[l], cmp_w2[l])
        kc, vc, kst, vsx, kwt, vwx, dk_t, dv = _nsa_prep(proj, w1_blk, w2_blk, _posterm(pos[l], w1_b[l]),
                                                         kgain2[l][None], batch)
        o_a = _nsa_attention(proj, kc, vc, kst, vsx, kwt, vwx,
                             bias_c, bias_t, bias_w, overlap, expand, gate_expand, q_gain[l][None], batch)
        o_d = _stickbreak(proj, dk_t, dv, tri, batch)
        o_b, o_c = _conv_sgu(proj, conv_w[l], sgu_w[l], sgu_bx[l])
        x2 = _outproj(o_a, o_b, o_c, o_d, group_gain[l].reshape(4, GROUP_WIDTH), w_out_b, l, x2)
        x2 = _ffn(x2, norm_ffn[l][None], wg_b, wu_b, wd_b, l)
    return x2.reshape(batch, seq, d_model)
```
